```python
import math
import jax, jax.numpy as jnp
from jax import lax
import numpy as np

D_MODEL = 4096
BATCH = 2
SEQ = 4096
DEPTH = 1
DEC_BATCH = 128
DEC_SEQ = 4
PAST_LEN = 2048
PAGE_SIZE = 128

F32 = jnp.float32
NEG_INF = -1e30
RMS_EPS = 1e-6
HEAD_DIM = 128
HG_HEADS = 3 * D_MODEL // (4 * HEAD_DIM)
HG_DK = 128
HG_DV = HEAD_DIM
HG_K = HG_HEADS * HG_DK
HG_V = HG_HEADS * HG_DV
HG_CHUNK = 16
DIL_PATTERNS = ((128, 1), (512, 4), (2048, 16))
N_PAT = len(DIL_PATTERNS)
ATT_SLOTS = D_MODEL // (4 * HEAD_DIM)
ATT_HEADS = N_PAT * ATT_SLOTS
ATT_WIDTH = ATT_HEADS * HEAD_DIM
ATT_OUT = ATT_SLOTS * HEAD_DIM
MIX_WIDTH = HG_V + ATT_OUT
SPLITS = (HG_K, 2 * HG_K, 2 * HG_K + HG_V, 2 * HG_K + 2 * HG_V,
          2 * HG_K + 2 * HG_V + ATT_WIDTH, 2 * HG_K + 2 * HG_V + 2 * ATT_WIDTH)
IN_COLS = 2 * HG_K + 2 * HG_V + 3 * ATT_WIDTH
ALIBI_MAX = 8.0
MEM_LEN = 256
MEM_HEADS = 4
MEM_HEAD_DIM = D_MODEL // MEM_HEADS
PK_HEADS = 8
N_KEYS = 128
N_EXPERTS = N_KEYS * N_KEYS
PK_DIM = 256
PK_HALF = PK_DIM // 2
PK_TOPK = 16
PEER_BLOCK = 64

kernel_name = 'hybrid_hgrn2_dilated_peer_step'


def rmsnorm(x, g):
    xf = x.astype(F32)
    y = xf * lax.rsqrt(jnp.mean(xf * xf, axis=-1, keepdims=True) + RMS_EPS)
    return (y * g.astype(F32)).astype(x.dtype)


def alibi_slopes():
    k = jnp.arange(1, ATT_HEADS + 1, dtype=F32)
    return (2.0 ** (-ALIBI_MAX * k / ATT_HEADS)).reshape(N_PAT, ATT_SLOTS)


def mixer_projection(n, w_in):
    b, L, _ = n.shape
    hq, hf, hi, hg, aq, ak, av = jnp.split(n @ w_in, SPLITS, axis=-1)
    att = lambda a: a.reshape(b, L, N_PAT, ATT_SLOTS, HEAD_DIM)
    return (hq, hf, hi, hg), (att(aq), att(ak), att(av))


def hgrn2_recurrence(q, k, v, logf, s0):
    b, L, H, K = q.shape
    c = math.gcd(L, HG_CHUNK)
    nc = L // c
    to_chunks = lambda a: jnp.moveaxis(a.reshape(b, nc, c, H, a.shape[-1]), 1, 0)
    causal = jnp.tril(jnp.ones((c, c), dtype=bool))

    def step(S, inp):
        qc, kc, vc, gc = inp
        cum = jnp.cumsum(gc, axis=1)
        last = cum[:, -1]
        q_in = qc * jnp.exp(cum)
        k_in = kc * jnp.exp(-cum)
        a = jnp.where(causal, jnp.einsum('bthk,bshk->bhts', q_in, k_in), 0.0)
        o = jnp.einsum('bhts,bshv->bthv', a, vc) + jnp.einsum('bthk,bhkv->bthv', q_in, S)
        k_out = kc * jnp.exp(last[:, None] - cum)
        S = jnp.exp(last)[..., None] * S + jnp.einsum('bshk,bshv->bhkv', k_out, vc)
        return S, o

    S, o = lax.scan(step, s0.astype(F32), (to_chunks(q), to_chunks(k), to_chunks(v), to_chunks(logf)))
    o = jnp.moveaxis(o, 0, 1).reshape(b, L, H, v.shape[-1])
    return o, S


def hgrn2_group(hq, hf, hi, hgate, lb, norm_g, s0):
    b, L, _ = hq.shape
    heads = lambda a, d: a.reshape(b, L, HG_HEADS, d).astype(F32)
    q = jax.nn.silu(heads(hq, HG_DK))
    f = lb + (1.0 - lb) * jax.nn.sigmoid(heads(hf, HG_DK))
    o, S = hgrn2_recurrence(q, 1.0 - f, heads(hi, HG_DV), jnp.log(f), s0)
    o = rmsnorm(o, norm_g) * jax.nn.silu(heads(hgate, HG_DV))
    return o.reshape(b, L, HG_V).astype(hq.dtype), S


def residue_split(a, dil):
    b, S = a.shape[:2]
    a = jnp.swapaxes(a.reshape(b, S // dil, dil, *a.shape[2:]), 1, 2)
    return a.reshape(b * dil, S // dil, *a.shape[3:])


def residue_merge(a, b, dil):
    L = a.shape[1]
    a = jnp.swapaxes(a.reshape(b, dil, L, *a.shape[2:]), 1, 2)
    return a.reshape(b, L * dil, *a.shape[3:])


def band_attention(q, k, v, band, step_slope):
    N, L, H, D = q.shape
    nb = -(-L // band)
    pad = nb * band - L
    blocks = lambda a: jnp.pad(a.astype(F32), ((0, 0), (0, pad), (0, 0), (0, 0))).reshape(N, nb, band, H, D)
    qb, kb, vb = blocks(q), blocks(k), blocks(v)
    prev = lambda a: jnp.concatenate([jnp.zeros_like(a[:, :1]), a[:, :-1]], axis=1)
    kk = jnp.concatenate([prev(kb), kb], axis=2)
    vv = jnp.concatenate([prev(vb), vb], axis=2)
    s = jnp.einsum('nbqhd,nbkhd->nbhqk', qb, kk) * (D ** -0.5)
    off = (jnp.arange(band)[:, None] + band) - jnp.arange(2 * band)[None, :]
    kpos = jnp.arange(nb)[:, None, None] * band + jnp.arange(2 * band)[None, None, :] - band
    valid = ((off >= 0) & (off <= band))[None] & (kpos >= 0)
    s = s - step_slope[:, None, None] * off.astype(F32)
    s = jnp.where(valid[None, :, None], s, NEG_INF)
    lse = jax.nn.logsumexp(s, axis=-1)
    p = jnp.exp(s - lse[..., None])
    o = jnp.einsum('nbhqk,nbkhd->nbqhd', p, vv).reshape(N, nb * band, H, D)[:, :L]
    lse = jnp.moveaxis(lse, 2, 3).reshape(N, nb * band, H)[:, :L]
    return o, lse


def merge_patterns(outs, lses):
    w = jax.nn.softmax(jnp.stack(lses, axis=0), axis=0)
    o = jnp.sum(w[..., None] * jnp.stack(outs, axis=0), axis=0)
    return o.reshape(*o.shape[:2], ATT_OUT)


def dilated_attention_prompt(q, k, v, slopes):
    b, S = q.shape[:2]
    outs, lses, bks, bvs = [], [], [], []
    for p, (win, dil) in enumerate(DIL_PATTERNS):
        o, lse = band_attention(residue_split(q[:, :, p], dil), residue_split(k[:, :, p], dil),
                                residue_split(v[:, :, p], dil), win // dil, slopes[p] * dil)
        outs.append(residue_merge(o, b, dil))
        lses.append(residue_merge(lse, b, dil))
        keep = min(win, S)
        bks.append(k[:, S - keep:, p])
        bvs.append(v[:, S - keep:, p])
    return merge_patterns(outs, lses).astype(q.dtype), bks, bvs


def dilated_attention_sample(q, k, v, bufs_k, bufs_v, slopes):
    n = q.shape[1]
    outs, lses, nks, nvs = [], [], [], []
    for p, (win, dil) in enumerate(DIL_PATTERNS):
        keep = bufs_k[p].shape[1]
        kc = jnp.concatenate([bufs_k[p], k[:, :, p].astype(bufs_k[p].dtype)], axis=1)
        vc = jnp.concatenate([bufs_v[p], v[:, :, p].astype(bufs_v[p].dtype)], axis=1)
        steps = jnp.arange(win // dil + 1)
        idx = keep + jnp.arange(n)[:, None] - dil * steps[None, :]
        valid = idx >= 0
        idx = jnp.maximum(idx, 0)
        kg = kc[:, idx].astype(F32)
        vg = vc[:, idx].astype(F32)
        s = jnp.einsum('bqhd,bqjhd->bhqj', q[:, :, p].astype(F32), kg) * (HEAD_DIM ** -0.5)
        s = s - (slopes[p] * dil)[:, None, None] * steps.astype(F32)
        s = jnp.where(valid, s, NEG_INF)
        lse = jax.nn.logsumexp(s, axis=-1)
        pr = jnp.exp(s - lse[..., None])
        outs.append(jnp.einsum('bhqj,bqjhd->bqhd', pr, vg))
        lses.append(jnp.swapaxes(lse, 1, 2))
        nks.append(kc[:, kc.shape[1] - keep:])
        nvs.append(vc[:, vc.shape[1] - keep:])
    return merge_patterns(outs, lses).astype(q.dtype), nks, nvs


def memory_kv(mem, g, wk, wv):
    bm, M, _ = mem.shape
    m = rmsnorm(mem, g)
    return ((m @ wk).reshape(bm, M, MEM_HEADS, MEM_HEAD_DIM), (m @ wv).reshape(bm, M, MEM_HEADS, MEM_HEAD_DIM))


def memory_cross_attention(n, mk, mv, wq, wo):
    b, L, _ = n.shape
    q = (n @ wq).reshape(b, L, MEM_HEADS, MEM_HEAD_DIM).astype(F32)
    s = jnp.einsum('bthd,bmhd->bhtm', q, mk.astype(F32)) * (MEM_HEAD_DIM ** -0.5)
    p = jax.nn.softmax(s, axis=-1)
    o = jnp.einsum('bhtm,bmhd->bthd', p, mv.astype(F32)).reshape(b, L, D_MODEL)
    return o.astype(n.dtype) @ wo


def peer_ffn(n, wq, subkeys, u, v):
    lead = n.shape[:-1]
    x = n.reshape(-1, D_MODEL)
    T = x.shape[0]
    nblk = -(-T // PEER_BLOCK)
    x = jnp.pad(x, ((0, nblk * PEER_BLOCK - T), (0, 0))).reshape(nblk, PEER_BLOCK, D_MODEL)

    def one_block(xt):
        qv = (xt @ wq).astype(F32).reshape(PEER_BLOCK, PK_HEADS, 2, PK_HALF)
        sc = jnp.einsum('thcd,hcnd->thcn', qv, subkeys.astype(F32))
        s_top, i_top = lax.top_k(sc, PK_TOPK)
        cand = (s_top[:, :, 0, :, None] + s_top[:, :, 1, None, :]).reshape(PEER_BLOCK, PK_HEADS, PK_TOPK * PK_TOPK)
        cidx = (i_top[:, :, 0, :, None] * N_KEYS + i_top[:, :, 1, None, :]).reshape(PEER_BLOCK, PK_HEADS, PK_TOPK * PK_TOPK)
        best, pos = lax.top_k(cand, PK_TOPK)
        eidx = jnp.take_along_axis(cidx, pos, axis=-1)
        gate = jax.nn.softmax(best, axis=-1)
        act = jax.nn.gelu(jnp.einsum('thkd,td->thk', u[eidx], xt).astype(F32), approximate=False)
        return jnp.einsum('thk,thkd->td', (gate * act).astype(v.dtype), v[eidx])

    out = lax.map(one_block, x).reshape(-1, D_MODEL)[:T]
    return out.reshape(*lead, D_MODEL).astype(n.dtype)


def setup_inputs(seed: int = 0) -> dict:
    key = jax.random.key(seed)
    ks = iter(jax.random.split(key, 40))
    nrm = lambda shape, scale: scale * jax.random.normal(next(ks), shape, F32)
    gain = lambda shape: 1.0 + 0.02 * jax.random.normal(next(ks), shape, F32)
    L = DEPTH
    wshape = lambda p: (L, DEC_BATCH, min(DIL_PATTERNS[p][0], PAST_LEN), ATT_SLOTS, HEAD_DIM)
    return {
        'x_prompt': nrm((BATCH, SEQ, D_MODEL), 1.0),
        'x_sample': nrm((DEC_BATCH, DEC_SEQ, D_MODEL), 1.0),
        'mem_prompt': nrm((BATCH, MEM_LEN, D_MODEL), 1.0),
        'state_hgrn': nrm((L, DEC_BATCH, HG_HEADS, HG_DK, HG_DV), 0.5),
        'cache_w1_k': nrm(wshape(0), 1.0),
        'cache_w1_v': nrm(wshape(0), 1.0),
        'cache_w2_k': nrm(wshape(1), 1.0),
        'cache_w2_v': nrm(wshape(1), 1.0),
        'cache_w3_k': nrm(wshape(2), 1.0),
        'cache_w3_v': nrm(wshape(2), 1.0),
        'cache_mem_k': nrm((L, DEC_BATCH, MEM_LEN, MEM_HEADS, MEM_HEAD_DIM), 1.0),
        'cache_mem_v': nrm((L, DEC_BATCH, MEM_LEN, MEM_HEADS, MEM_HEAD_DIM), 1.0),
        'norm_mix_g': gain((L, D_MODEL)),
        'w_in': nrm((L, D_MODEL, IN_COLS), D_MODEL ** -0.5),
        'hg_lower_bound': nrm((DEPTH + 1, HG_K), 0.5),
        'hg_norm_g': gain((L, HG_DV)),
        'w_out': nrm((L, MIX_WIDTH, D_MODEL), MIX_WIDTH ** -0.5),
        'norm_x_g': gain((L, D_MODEL)),
        'norm_mem_g': gain((L, D_MODEL)),
        'wq_x': nrm((L, D_MODEL, D_MODEL), D_MODEL ** -0.5),
        'wk_x': nrm((L, D_MODEL, D_MODEL), D_MODEL ** -0.5),
        'wv_x': nrm((L, D_MODEL, D_MODEL), D_MODEL ** -0.5),
        'wo_x': nrm((L, D_MODEL, D_MODEL), D_MODEL ** -0.5),
        'norm_ffn_g': gain((L, D_MODEL)),
        'peer_wq': nrm((L, D_MODEL, PK_HEADS * PK_DIM), D_MODEL ** -0.5),
        'peer_subkeys': nrm((L, PK_HEADS, 2, N_KEYS, PK_HALF), PK_HALF ** -0.5),
        'peer_u': nrm((L, N_EXPERTS, D_MODEL), D_MODEL ** -0.5),
        'peer_v': nrm((L, N_EXPERTS, D_MODEL), 0.5),
        'norm_final_g': gain((D_MODEL,)),
    }


def reference(x_prompt, x_sample, mem_prompt, state_hgrn, cache_w1_k, cache_w1_v, cache_w2_k, cache_w2_v,
              cache_w3_k, cache_w3_v, cache_mem_k, cache_mem_v, norm_mix_g, w_in, hg_lower_bound, hg_norm_g,
              w_out, norm_x_g, norm_mem_g, wq_x, wk_x, wv_x, wo_x, norm_ffn_g, peer_wq, peer_subkeys,
              peer_u, peer_v, norm_final_g):
    slopes = alibi_slopes()
    lbs = jnp.cumsum(jax.nn.softmax(hg_lower_bound.astype(F32), axis=0), axis=0)
    cache_k = (cache_w1_k, cache_w2_k, cache_w3_k)
    cache_v = (cache_w1_v, cache_w2_v, cache_w3_v)
    b = x_prompt.shape[0]
    hp, hs = x_prompt, x_sample
    hgp, hgs, mkp, mvp = [], [], [], []
    wkp = [[] for _ in DIL_PATTERNS]
    wvp = [[] for _ in DIL_PATTERNS]
    wks = [[] for _ in DIL_PATTERNS]
    wvs = [[] for _ in DIL_PATTERNS]
    for l in range(DEPTH):
        lb = lbs[l].reshape(HG_HEADS, HG_DK)
        hg_in, (q, k, v) = mixer_projection(rmsnorm(hp, norm_mix_g[l]), w_in[l])
        o_hg, s_p = hgrn2_group(*hg_in, lb, hg_norm_g[l], jnp.zeros((b, HG_HEADS, HG_DK, HG_DV), F32))
        o_at, bk, bv = dilated_attention_prompt(q, k, v, slopes)
        hp = hp + jnp.concatenate([o_hg.astype(hp.dtype), o_at.astype(hp.dtype)], axis=-1) @ w_out[l]
        mk, mv = memory_kv(mem_prompt, norm_mem_g[l], wk_x[l], wv_x[l])
        hp = hp + memory_cross_attention(rmsnorm(hp, norm_x_g[l]), mk, mv, wq_x[l], wo_x[l])
        hp = hp + peer_ffn(rmsnorm(hp, norm_ffn_g[l]), peer_wq[l], peer_subkeys[l], peer_u[l], peer_v[l])
        hgp.append(s_p.astype(x_prompt.dtype))
        mkp.append(mk)
        mvp.append(mv)
        hg_in, (q, k, v) = mixer_projection(rmsnorm(hs, norm_mix_g[l]), w_in[l])
        o_hg, s_s = hgrn2_group(*hg_in, lb, hg_norm_g[l], state_hgrn[l])
        o_at, nk, nv = dilated_attention_sample(q, k, v, [c[l] for c in cache_k], [c[l] for c in cache_v], slopes)
        hs = hs + jnp.concatenate([o_hg.astype(hs.dtype), o_at.astype(hs.dtype)], axis=-1) @ w_out[l]
        hs = hs + memory_cross_attention(rmsnorm(hs, norm_x_g[l]), cache_mem_k[l], cache_mem_v[l], wq_x[l], wo_x[l])
        hs = hs + peer_ffn(rmsnorm(hs, norm_ffn_g[l]), peer_wq[l], peer_subkeys[l], peer_u[l], peer_v[l])
        hgs.append(s_s.astype(state_hgrn.dtype))
        for p in range(N_PAT):
            wkp[p].append(bk[p])
            wvp[p].append(bv[p])
            wks[p].append(nk[p])
            wvs[p].append(nv[p])
    y_prompt = rmsnorm(hp, norm_final_g)
    y_sample = rmsnorm(hs, norm_final_g)
    hgrn_prompt = jnp.stack(hgp)
    w1k_prompt, w1v_prompt = jnp.stack(wkp[0]), jnp.stack(wvp[0])
    w2k_prompt, w2v_prompt = jnp.stack(wkp[1]), jnp.stack(wvp[1])
    w3k_prompt, w3v_prompt = jnp.stack(wkp[2]), jnp.stack(wvp[2])
    memk_prompt, memv_prompt = jnp.stack(mkp), jnp.stack(mvp)
    hgrn_sample = jnp.stack(hgs)
    w1k_sample, w1v_sample = jnp.stack(wks[0]), jnp.stack(wvs[0])
    w2k_sample, w2v_sample = jnp.stack(wks[1]), jnp.stack(wvs[1])
    w3k_sample, w3v_sample = jnp.stack(wks[2]), jnp.stack(wvs[2])
    return (y_prompt, y_sample, hgrn_prompt, w1k_prompt, w1v_prompt, w2k_prompt, w2v_prompt, w3k_prompt, w3v_prompt,
            memk_prompt, memv_prompt, hgrn_sample, w1k_sample, w1v_sample, w2k_sample, w2v_sample, w3k_sample, w3v_sample)
```

```python
import functools
import math

import jax
import jax.numpy as jnp
from jax import lax
from jax.experimental import pallas as pl
from jax.experimental.pallas import tpu as pltpu

F32 = jnp.float32
BF16 = jnp.bfloat16
NEG_INF = -1e30
RMS_EPS = 1e-6
HEAD_DIM = 128
HG_CHUNK = 16
DIL_PATTERNS = ((128, 1), (512, 4), (2048, 16))
ALIBI_MAX = 8.0
MEM_HEADS = 4
PK_HEADS = 8
N_KEYS = 128
PK_TOPK = 16

VMEM_LIMIT_BYTES = 56 * 1024 * 1024


def _params(*sem):
    return pltpu.CompilerParams(dimension_semantics=sem, vmem_limit_bytes=VMEM_LIMIT_BYTES)


def _row_tile(n, target):
    best = None
    for t in range(8, min(n, target) + 1, 8):
        if n % t == 0:
            best = t
    assert best is not None, (n, target)
    return best


def _rmsnorm_kernel(x_ref, g_ref, o_ref):
    x = x_ref[...]
    ms = jnp.mean(x * x, axis=-1, keepdims=True)
    o_ref[...] = (x * lax.rsqrt(ms + RMS_EPS) * g_ref[...]).astype(o_ref.dtype)


def rmsnorm(x, g, out_dtype=BF16, tm=512):
    T, D = x.shape
    tm = _row_tile(T, tm)
    return pl.pallas_call(
        _rmsnorm_kernel,
        out_shape=jax.ShapeDtypeStruct((T, D), out_dtype),
        grid=(T // tm,),
        in_specs=[pl.BlockSpec((tm, D), lambda i: (i, 0)), pl.BlockSpec((1, D), lambda i: (0, 0))],
        out_specs=pl.BlockSpec((tm, D), lambda i: (i, 0)),
        compiler_params=_params("parallel"),
        name="rmsnorm",
    )(x, g.reshape(1, D).astype(F32))


def _add_rmsnorm_kernel(x_ref, y_ref, g_ref, o_ref, *, normalize):
    x = x_ref[...] + y_ref[...]
    if normalize:
        ms = jnp.mean(x * x, axis=-1, keepdims=True)
        x = x * lax.rsqrt(ms + RMS_EPS) * g_ref[...]
    o_ref[...] = x.astype(o_ref.dtype)


def add_rmsnorm(x, y, g, tm=256):
    T, D = x.shape
    tm = _row_tile(T, tm)
    gain = jnp.ones((1, D), F32) if g is None else g.reshape(1, D).astype(F32)
    return pl.pallas_call(
        functools.partial(_add_rmsnorm_kernel, normalize=g is not None),
        out_shape=jax.ShapeDtypeStruct((T, D), F32),
        grid=(T // tm,),
        in_specs=[pl.BlockSpec((tm, D), lambda i: (i, 0)), pl.BlockSpec((tm, D), lambda i: (i, 0)),
                  pl.BlockSpec((1, D), lambda i: (0, 0))],
        out_specs=pl.BlockSpec((tm, D), lambda i: (i, 0)),
        compiler_params=_params("parallel"),
        name="add_rmsnorm",
    )(x, y, gain)


def _mm_kernel(*refs, k_splits, has_res):
    n_a = len(k_splits)
    a_refs, w_ref = refs[:n_a], refs[n_a]
    res_ref = refs[n_a + 1] if has_res else None
    o_ref = refs[-1]
    acc, off = None, 0
    for a_ref, kk in zip(a_refs, k_splits):
        part = jnp.dot(a_ref[...], w_ref[off:off + kk, :].astype(BF16), preferred_element_type=F32)
        acc = part if acc is None else acc + part
        off += kk
    if has_res:
        acc = acc + res_ref[...]
    o_ref[...] = acc.astype(o_ref.dtype)


def matmul(a_list, w, res=None, out_dtype=F32, tm=1088, tn=512):
    M = a_list[0].shape[0]
    k_splits = tuple(a.shape[1] for a in a_list)
    K, N = w.shape
    assert sum(k_splits) == K
    tm, tn = _row_tile(M, tm), min(tn, N)
    assert N % tn == 0
    in_specs = [pl.BlockSpec((tm, kk), lambda i, j: (i, 0)) for kk in k_splits]
    in_specs.append(pl.BlockSpec((K, tn), lambda i, j: (0, j)))
    args = list(a_list) + [w]
    if res is not None:
        in_specs.append(pl.BlockSpec((tm, tn), lambda i, j: (i, j)))
        args.append(res)
    return pl.pallas_call(
        functools.partial(_mm_kernel, k_splits=k_splits, has_res=res is not None),
        out_shape=jax.ShapeDtypeStruct((M, N), out_dtype),
        grid=(M // tm, N // tn),
        in_specs=in_specs,
        out_specs=pl.BlockSpec((tm, tn), lambda i, j: (i, j)),
        compiler_params=_params("parallel", "arbitrary"),
        name="matmul",
    )(*args)


def _silu(x):
    return x * jax.nn.sigmoid(x)


def _split3(x):
    hi = x.astype(BF16)
    r1 = x - hi.astype(F32)
    mid = r1.astype(BF16)
    lo = (r1 - mid.astype(F32)).astype(BF16)
    return hi, mid, lo


def _rows_bcast(x, row_ids, reps):
    W = x.shape[1]
    parts = []
    for r in row_ids:
        row = jnp.zeros((1, W), x.dtype) if r < 0 else x[r:r + 1, :]
        parts.append(jnp.broadcast_to(row, (reps, W)))
    return parts[0] if len(parts) == 1 else jnp.concatenate(parts, axis=0)


def _dot_nt(a, b):
    return lax.dot_general(a, b, (((1,), (1,)), ((), ())), preferred_element_type=F32)


def _dot_tn(a, b):
    return lax.dot_general(a, b, (((0,), (0,)), ((), ())), preferred_element_type=F32)


def _hgrn_chunk(q, k, g, v, st, leaf):
    C = q.shape[0]
    row = lax.broadcasted_iota(jnp.int32, (C, q.shape[1]), 0)
    it = lax.broadcasted_iota(jnp.int32, (C, C), 0)
    js = lax.broadcasted_iota(jnp.int32, (C, C), 1)
    tri = jnp.where(it >= js, 1.0, 0.0).astype(BF16)
    g_hi, g_mid, g_lo = _split3(g)
    cum = (jnp.dot(tri, g_hi, preferred_element_type=F32) + jnp.dot(tri, g_mid, preferred_element_type=F32)
           + jnp.dot(tri, g_lo, preferred_element_type=F32))
    last = cum[C - 1:C, :]

    a = jnp.zeros((C, C), F32)
    bs = C
    while bs > leaf:
        half = bs // 2
        upper = (row % bs) >= half
        cmid = _rows_bcast(cum, [b0 + half - 1 for b0 in range(0, C, bs)], bs)
        e = jnp.exp(jnp.where(upper, cum - cmid, cmid - cum))
        q_l = jnp.where(upper, q * e, 0.0).astype(BF16)
        k_l = jnp.where(upper, 0.0, k * e).astype(BF16)
        a = a + jnp.where((it // bs) == (js // bs), _dot_nt(q_l, k_l), 0.0)
        bs = half
    cstart = _rows_bcast(cum, [b0 - 1 for b0 in range(0, C, leaf)], leaf)
    q_f = (q * jnp.exp(cum - cstart)).astype(BF16)
    k_f = (k * jnp.exp(cstart - cum)).astype(BF16)
    a = a + jnp.where(((it // leaf) == (js // leaf)) & (it >= js), _dot_nt(q_f, k_f), 0.0)

    vb = v.astype(BF16)
    q_g = (q * jnp.exp(cum)).astype(BF16)
    o = jnp.dot(a.astype(BF16), vb, preferred_element_type=F32) + _dot_nt(q_g, st.astype(BF16))
    k_e = (k * jnp.exp(last - cum)).astype(BF16)
    st_new = st * jnp.exp(last) + _dot_tn(vb, k_e)
    return o, st_new


def _hgrn_gates(hq, hf, lb):
    q = _silu(hq)
    f = lb + (1.0 - lb) * jax.nn.sigmoid(hf)
    return q, 1.0 - f, jnp.log(f)


def _hgrn_out(o, hgate, ng):
    ms = jnp.mean(o * o, axis=-1, keepdims=True)
    return o * lax.rsqrt(ms + RMS_EPS) * ng * _silu(hgate)


def _hgrn_prompt_kernel(hq_ref, hf_ref, hi_ref, hg_ref, lb_ref, ng_ref, o_ref, s_ref, st_scr, *, chunk, leaf):
    c = pl.program_id(2)

    @pl.when(c == 0)
    def _():
        st_scr[...] = jnp.zeros_like(st_scr)

    rows = hq_ref.shape[0]
    lb, ng = lb_ref[...], ng_ref[...]
    for s0 in range(0, rows, chunk):
        sl = slice(s0, s0 + chunk)
        q, k, g = _hgrn_gates(hq_ref[sl, :], hf_ref[sl, :], lb)
        o, st_new = _hgrn_chunk(q, k, g, hi_ref[sl, :], st_scr[...], leaf)
        st_scr[...] = st_new
        o_ref[sl, :] = _hgrn_out(o, hg_ref[sl, :], ng).astype(o_ref.dtype)

    @pl.when(c == pl.num_programs(2) - 1)
    def _():
        s_ref[0, 0] = st_scr[...].T


def hgrn_prompt(proj, lb, ng, b, L, n_heads, col_blocks, rows_per_step=512, chunk=128, leaf=HG_CHUNK):
    rows = min(rows_per_step, L)
    assert L % rows == 0 and rows % chunk == 0
    nc = L // rows

    def in_spec(cb):
        return pl.BlockSpec((rows, HEAD_DIM), lambda bi, h, c: (bi * nc + c, cb + h))

    return pl.pallas_call(
        functools.partial(_hgrn_prompt_kernel, chunk=chunk, leaf=leaf),
        out_shape=(jax.ShapeDtypeStruct((b * L, n_heads * HEAD_DIM), BF16),
                   jax.ShapeDtypeStruct((b, n_heads, HEAD_DIM, HEAD_DIM), F32)),
        grid=(b, n_heads, nc),
        in_specs=[in_spec(cb) for cb in col_blocks] + [
            pl.BlockSpec((1, HEAD_DIM), lambda bi, h, c: (0, h)),
            pl.BlockSpec((1, HEAD_DIM), lambda bi, h, c: (0, 0))],
        out_specs=(pl.BlockSpec((rows, HEAD_DIM), lambda bi, h, c: (bi * nc + c, h)),
                   pl.BlockSpec((1, 1, HEAD_DIM, HEAD_DIM), lambda bi, h, c: (bi, h, 0, 0))),
        scratch_shapes=[pltpu.VMEM((HEAD_DIM, HEAD_DIM), F32)],
        compiler_params=_params("parallel", "parallel", "arbitrary"),
        name="hgrn_prompt",
    )(proj, proj, proj, proj, lb, ng)


def _alibi_slopes(n_pat, n_slots):
    k = jnp.arange(1, n_pat * n_slots + 1, dtype=F32)
    return (2.0 ** (-ALIBI_MAX * k / (n_pat * n_slots))).reshape(n_pat, n_slots)


def _strided_rows(ref, start, size, stride):
    if stride == 1:
        return ref[pl.ds(start, size), :]
    return ref[pl.ds(start, size, stride=stride), :]


def _band_softmax(q, kk, vv, bias, valid, scale):
    s = _dot_nt(q.astype(BF16), kk.astype(BF16)) * scale - bias
    s = jnp.where(valid, s, NEG_INF)
    m = jnp.max(s, axis=-1, keepdims=True)
    p = jnp.exp(s - m)
    l = jnp.sum(p, axis=-1, keepdims=True)
    o = jnp.dot(p.astype(BF16), vv.astype(BF16), preferred_element_type=F32) / l
    return o, m + jnp.log(l)


def _dil_attn_kernel(q_ref, kp_ref, kc_ref, vp_ref, vc_ref, slope_ref, o_ref, lse_ref, *, dil, band, units):
    first = pl.program_id(2) == 0
    span = band * dil
    scale = HEAD_DIM ** -0.5
    qi = lax.broadcasted_iota(jnp.int32, (band, 2 * band), 0)
    kj = lax.broadcasted_iota(jnp.int32, (band, 2 * band), 1)
    off = qi + band - kj
    in_band = (off >= 0) & (off <= band)
    bias = slope_ref[0] * float(dil) * off.astype(F32)
    for u in range(units):
        for r in range(dil):
            base = u * span + r
            q = _strided_rows(q_ref, base, band, dil)
            if u == 0:
                kp, vp = _strided_rows(kp_ref, r, band, dil), _strided_rows(vp_ref, r, band, dil)
                valid = in_band & (jnp.logical_not(first) | (kj >= band))
            else:
                kp, vp = _strided_rows(kc_ref, base - span, band, dil), _strided_rows(vc_ref, base - span, band, dil)
                valid = in_band
            kk = jnp.concatenate([kp, _strided_rows(kc_ref, base, band, dil)], axis=0)
            vv = jnp.concatenate([vp, _strided_rows(vc_ref, base, band, dil)], axis=0)
            o, lse = _band_softmax(q, kk, vv, bias, valid, scale)
            lse_b = jnp.broadcast_to(lse, (band, HEAD_DIM))
            if dil == 1:
                o_ref[pl.ds(base, band), :] = o
                lse_ref[pl.ds(base, band), :] = lse_b
            else:
                o_ref[pl.ds(base, band, stride=dil), :] = o
                lse_ref[pl.ds(base, band, stride=dil), :] = lse_b


def dilated_attention_prompt(proj, slopes_p, b, L, n_slots, qcol, kcol, vcol, win, dil, units):
    band = win // dil
    span = band * dil
    rows = span * units
    assert L % rows == 0
    nblk = L // rows
    cur = lambda c0: pl.BlockSpec((rows, HEAD_DIM), lambda bi, h, i: (bi * nblk + i, c0 + h))
    prev = lambda c0: pl.BlockSpec(
        (span, HEAD_DIM), lambda bi, h, i: (bi * nblk * units + jnp.maximum(i * units - 1, 0), c0 + h))
    out = pl.BlockSpec((rows, HEAD_DIM), lambda bi, h, i: (bi * nblk + i, h))
    shape = jax.ShapeDtypeStruct((b * L, n_slots * HEAD_DIM), F32)
    slope_rep = jnp.broadcast_to(slopes_p.reshape(n_slots, 1, 1), (n_slots, 1, 2 * band)).astype(F32)
    return pl.pallas_call(
        functools.partial(_dil_attn_kernel, dil=dil, band=band, units=units),
        out_shape=(shape, shape),
        grid=(b, n_slots, nblk),
        in_specs=[cur(qcol), prev(kcol), cur(kcol), prev(vcol), cur(vcol),
                  pl.BlockSpec((1, 1, 2 * band), lambda bi, h, i: (h, 0, 0))],
        out_specs=(out, out),
        compiler_params=_params("parallel", "parallel", "arbitrary"),
        name=f"dil_attn_prompt_d{dil}",
    )(proj, proj, proj, proj, proj, slope_rep)


def _merge_kernel(*refs):
    n = (len(refs) - 1) // 2
    o_refs, l_refs, out_ref = refs[:n], refs[n:2 * n], refs[-1]
    ls = [r[...] for r in l_refs]
    m = functools.reduce(jnp.maximum, ls)
    ws = [jnp.exp(l - m) for l in ls]
    num = functools.reduce(lambda a, c: a + c, [w * r[...] for w, r in zip(ws, o_refs)])
    den = functools.reduce(lambda a, c: a + c, ws)
    out_ref[...] = (num / den).astype(out_ref.dtype)


def merge_patterns(outs, lses, tm=512):
    T, W = outs[0].shape
    tm = _row_tile(T, tm)
    spec = pl.BlockSpec((tm, W), lambda i: (i, 0))
    return pl.pallas_call(
        _merge_kernel,
        out_shape=jax.ShapeDtypeStruct((T, W), BF16),
        grid=(T // tm,),
        in_specs=[spec] * (2 * len(outs)),
        out_specs=spec,
        compiler_params=_params("parallel"),
        name="merge_patterns",
    )(*outs, *lses)


def _mem_attn_kernel(q_ref, k_ref, v_ref, o_ref):
    scale = q_ref.shape[-1] ** -0.5
    s = _dot_nt(q_ref[...].astype(BF16), k_ref[...].astype(BF16)) * scale
    m = jnp.max(s, axis=-1, keepdims=True)
    p = jnp.exp(s - m)
    l = jnp.sum(p, axis=-1, keepdims=True)
    o = jnp.dot(p.astype(BF16), v_ref[...].astype(BF16), preferred_element_type=F32) / l
    o_ref[...] = o.astype(o_ref.dtype)


def mem_attention_prompt(q, mk, mv, b, L, M, n_heads, tq=1024):
    D = mk.shape[1]
    hd = D // n_heads
    tq = _row_tile(L, tq)
    nq = L // tq
    return pl.pallas_call(
        _mem_attn_kernel,
        out_shape=jax.ShapeDtypeStruct((b * L, D), BF16),
        grid=(b, n_heads, nq),
        in_specs=[pl.BlockSpec((tq, hd), lambda bi, h, i: (bi * nq + i, h)),
                  pl.BlockSpec((M, hd), lambda bi, h, i: (bi, h)),
                  pl.BlockSpec((M, hd), lambda bi, h, i: (bi, h))],
        out_specs=pl.BlockSpec((tq, hd), lambda bi, h, i: (bi * nq + i, h)),
        compiler_params=_params("parallel", "parallel", "arbitrary"),
        name="mem_attn_prompt",
    )(q, mk, mv)


def _dot_nt_x3(a, b):
    a_hi, b_hi = a.astype(BF16), b.astype(BF16)
    a_lo, b_lo = (a - a_hi.astype(F32)).astype(BF16), (b - b_hi.astype(F32)).astype(BF16)
    return _dot_nt(a_hi, b_hi) + _dot_nt(a_hi, b_lo) + _dot_nt(a_lo, b_hi)


def _topk_rows(x, k, payload=None):
    n = x.shape[0]
    iota = lax.broadcasted_iota(jnp.int32, x.shape, 0).astype(F32)
    vals, outs = [], []
    for _ in range(k):
        m = jnp.max(x, axis=0, keepdims=True)
        pos = jnp.min(jnp.where(x == m, iota, float(n)), axis=0, keepdims=True)
        sel = iota == pos
        vals.append(m)
        outs.append(pos if payload is None else jnp.max(jnp.where(sel, payload, -1.0), axis=0, keepdims=True))
        x = jnp.where(sel, -jnp.inf, x)
    return jnp.concatenate(vals, axis=0), jnp.concatenate(outs, axis=0)


_STAGE_PITCH = N_KEYS + 8


def _router_kernel(qp_ref, sk_ref, g_ref, ii_scr, ij_scr, w_scr, iit_scr, ijt_scr, wt_scr, stage_scr):
    tb = qp_ref.shape[0]
    kk = PK_TOPK
    for h in range(PK_HEADS):
        tops = []
        for c in range(2):
            col = (2 * h + c) * HEAD_DIM
            sc = _dot_nt_x3(sk_ref[2 * h + c], qp_ref[:, col:col + HEAD_DIM])
            tops.append(_topk_rows(sc, kk))
        (v1, i1), (v2, i2) = tops
        cand = jnp.concatenate([v1[a:a + 1, :] + v2 for a in range(kk)], axis=0)
        cidx = jnp.concatenate([i1[a:a + 1, :] * float(N_KEYS) + i2 for a in range(kk)], axis=0)
        best, eidx = _topk_rows(cand, kk, payload=cidx)
        e = jnp.exp(best - best[0:1, :])
        gate = e / jnp.sum(e, axis=0, keepdims=True)
        ei = jnp.floor(eidx * (1.0 / N_KEYS))
        ii_scr[h * kk:(h + 1) * kk, :] = ei
        ij_scr[h * kk:(h + 1) * kk, :] = eidx - ei * float(N_KEYS)
        w_scr[h * kk:(h + 1) * kk, :] = gate
    iit_scr[...] = ii_scr[...].T
    ijt_scr[...] = ij_scr[...].T
    wt_scr[...] = w_scr[...].T
    sub = lax.broadcasted_iota(jnp.int32, (N_KEYS, PK_HEADS * kk), 0).astype(F32)

    def per_token(t, carry):
        a = jnp.where(sub == iit_scr[pl.ds(t, 1), :], wt_scr[pl.ds(t, 1), :], 0.0).astype(BF16)
        bsel = jnp.where(sub == ijt_scr[pl.ds(t, 1), :], 1.0, 0.0).astype(BF16)
        row0 = pl.multiple_of(t * _STAGE_PITCH, 8)
        stage_scr[pl.ds(row0, N_KEYS), :] = _dot_nt(a, bsel)
        return carry

    lax.fori_loop(0, tb, per_token, 0)
    for i in range(N_KEYS):
        g_ref[:, i * N_KEYS:(i + 1) * N_KEYS] = stage_scr[pl.ds(i, tb, stride=_STAGE_PITCH), :].astype(g_ref.dtype)


def peer_router(qp, subkeys, tb=128):
    T = qp.shape[0]
    assert T % tb == 0 and tb == N_KEYS
    sk = subkeys.reshape(PK_HEADS * 2, N_KEYS, HEAD_DIM)
    slots = PK_HEADS * PK_TOPK
    return pl.pallas_call(
        _router_kernel,
        out_shape=jax.ShapeDtypeStruct((T, N_KEYS * N_KEYS), BF16),
        grid=(T // tb,),
        in_specs=[pl.BlockSpec((tb, qp.shape[1]), lambda i: (i, 0)),
                  pl.BlockSpec(sk.shape, lambda i: (0, 0, 0))],
        out_specs=pl.BlockSpec((tb, N_KEYS * N_KEYS), lambda i: (i, 0)),
        scratch_shapes=[pltpu.VMEM((slots, tb), F32)] * 3 + [pltpu.VMEM((tb, slots), F32)] * 3
        + [pltpu.VMEM((tb * _STAGE_PITCH, N_KEYS), F32)],
        compiler_params=_params("parallel"),
        name="peer_router",
    )(qp, sk)


def _gelu(x):
    return 0.5 * x * (1.0 + lax.erf(x * (2.0 ** -0.5)))


def _peer_kernel(n_ref, g_ref, u_ref, v_ref, o_ref):
    j = pl.program_id(1)
    xu = _dot_nt(n_ref[...], u_ref[...].astype(BF16))
    hid = (g_ref[...].astype(F32) * _gelu(xu)).astype(BF16)
    part = jnp.dot(hid, v_ref[...].astype(BF16), preferred_element_type=F32)

    @pl.when(j == 0)
    def _():
        o_ref[...] = part

    @pl.when(j > 0)
    def _():
        o_ref[...] += part


def peer_experts(n, gates, u, v, tm=544, te=256):
    T, D = n.shape
    E = u.shape[0]
    tm = _row_tile(T, tm)
    assert E % te == 0
    return pl.pallas_call(
        _peer_kernel,
        out_shape=jax.ShapeDtypeStruct((T, D), F32),
        grid=(T // tm, E // te),
        in_specs=[pl.BlockSpec((tm, D), lambda i, j: (i, 0)),
                  pl.BlockSpec((tm, te), lambda i, j: (i, j)),
                  pl.BlockSpec((te, D), lambda i, j: (j, 0)),
                  pl.BlockSpec((te, D), lambda i, j: (j, 0))],
        out_specs=pl.BlockSpec((tm, D), lambda i, j: (i, 0)),
        compiler_params=_params("parallel", "arbitrary"),
        name="peer_experts",
    )(n, gates, u, v)


def _hgrn_sample_kernel(hq_ref, hf_ref, hi_ref, hg_ref, lb_ref, ng_ref, s_ref, o_ref, so_ref, *, n_tok):
    rows, width = hq_ref.shape
    n_seq, n_head = rows // n_tok, width // HEAD_DIM
    q, k, g = _hgrn_gates(hq_ref[...], hf_ref[...], lb_ref[...])
    t = lax.broadcasted_iota(jnp.int32, (rows, width), 0) % n_tok
    cum, sh = g, 1
    while sh < n_tok:
        cum = cum + jnp.where(t >= sh, pltpu.roll(cum, sh, axis=0), 0.0)
        sh *= 2
    last = _rows_bcast(cum, [s * n_tok + n_tok - 1 for s in range(n_seq)], n_tok)
    q_in = q * jnp.exp(cum)
    k_in = k * jnp.exp(-cum)
    k_out = k * jnp.exp(last - cum)
    dec_t = jnp.exp(last).T
    v = hi_ref[...]
    it = lax.broadcasted_iota(jnp.int32, (n_tok, n_tok), 0)
    js = lax.broadcasted_iota(jnp.int32, (n_tok, n_tok), 1)
    ng = ng_ref[...]
    for s in range(n_seq):
        r = slice(s * n_tok, (s + 1) * n_tok)
        for h in range(n_head):
            c = slice(h * HEAD_DIM, (h + 1) * HEAD_DIM)
            st = s_ref[0, s, h]
            qh, vh = q_in[r, c].astype(BF16), v[r, c].astype(BF16)
            a = jnp.where(it >= js, _dot_nt(qh, k_in[r, c].astype(BF16)), 0.0)
            o = jnp.dot(a.astype(BF16), vh, preferred_element_type=F32)
            o = o + jnp.dot(qh, st.astype(BF16), preferred_element_type=F32)
            dec = dec_t[c, s * n_tok:s * n_tok + 1]
            so_ref[0, s, h] = st * dec + _dot_tn(k_out[r, c].astype(BF16), vh)
            o_ref[r, c] = _hgrn_out(o, hg_ref[r, c], ng).astype(o_ref.dtype)


def hgrn_sample(proj, row0, state, lb, ng, n_tok, col_blocks, seqs_per_step=4, heads_per_step=4):
    _, n_seq, n_heads, _, _ = state.shape
    rows, width = seqs_per_step * n_tok, heads_per_step * HEAD_DIM
    assert n_seq % seqs_per_step == 0 and n_heads % heads_per_step == 0 and row0 % rows == 0
    r0, nhb = row0 // rows, n_heads // heads_per_step

    def in_spec(cb):
        return pl.BlockSpec((rows, width), lambda i, j: (r0 + i, cb // heads_per_step + j))

    st_spec = pl.BlockSpec((1, seqs_per_step, heads_per_step, HEAD_DIM, HEAD_DIM), lambda i, j: (0, i, j, 0, 0))
    return pl.pallas_call(
        functools.partial(_hgrn_sample_kernel, n_tok=n_tok),
        out_shape=(jax.ShapeDtypeStruct((n_seq * n_tok, n_heads * HEAD_DIM), BF16),
                   jax.ShapeDtypeStruct(state.shape, F32)),
        grid=(n_seq // seqs_per_step, nhb),
        in_specs=[in_spec(cb) for cb in col_blocks] + [
            pl.BlockSpec((1, width), lambda i, j: (0, j)),
            pl.BlockSpec((1, HEAD_DIM), lambda i, j: (0, 0)),
            st_spec],
        out_specs=(pl.BlockSpec((rows, width), lambda i, j: (i, j)), st_spec),
        compiler_params=_params("parallel", "parallel"),
        name="hgrn_sample",
    )(proj, proj, proj, proj, lb, ng, state)


def _cache_shift_kernel(c_ref, new_ref, o_ref):
    keep, n_new = c_ref.shape[2], new_ref.shape[1]
    o_ref[0, 0, 0:keep - n_new] = c_ref[0, 0, n_new:keep]
    o_ref[0, 0, keep - n_new:keep] = new_ref[0]


def cache_shift(cache, new):
    _, B, keep, H, D = cache.shape
    n = new.shape[1]
    if n >= keep:
        return new[None, :, n - keep:]
    blk = pl.BlockSpec((1, 1, keep, H, D), lambda i: (0, i, 0, 0, 0))
    return pl.pallas_call(
        _cache_shift_kernel,
        out_shape=jax.ShapeDtypeStruct(cache.shape, cache.dtype),
        grid=(B,),
        in_specs=[blk, pl.BlockSpec((1, n, H, D), lambda i: (i, 0, 0, 0))],
        out_specs=blk,
        compiler_params=_params("parallel"),
        name="cache_shift",
    )(cache, new)


def _lane_sum_rep(x):
    ones = jnp.ones((x.shape[1], HEAD_DIM), BF16)
    hi = x.astype(BF16)
    lo = (x - hi.astype(F32)).astype(BF16)
    return jnp.dot(hi, ones, preferred_element_type=F32) + jnp.dot(lo, ones, preferred_element_type=F32)


def _dil_attn_sample_kernel(q_ref, kn_ref, vn_ref, ck_ref, cv_ref, slope_ref, o_ref, lse_ref, *, dil, band):
    n, H, D = q_ref.shape[1:]
    keep = ck_ref.shape[2]
    scale = D ** -0.5
    slope = slope_ref[...] * float(dil)
    jm = lax.broadcasted_iota(jnp.int32, (band, H, D), 0)
    for i in range(n):
        q = q_ref[0, i]
        if dil >= n:
            start = keep + i - band * dil
            kc = ck_ref[0, 0, pl.ds(start, band, stride=dil)]
            vc = cv_ref[0, 0, pl.ds(start, band, stride=dil)]
            jc = band - jm
            ok_c = None
        else:
            assert dil == 1
            kc = ck_ref[0, 0, pl.ds(keep - band, band)]
            vc = cv_ref[0, 0, pl.ds(keep - band, band)]
            jc = band + i - jm
            ok_c = jc <= band
        s_c = _lane_sum_rep((kc * q[None]).reshape(band * H, D)).reshape(band, H, D) * scale
        s_c = s_c - slope[None] * jc.astype(F32)
        if ok_c is not None:
            s_c = jnp.where(ok_c, s_c, NEG_INF)
        news = [ip for ip in range(i + 1) if (i - ip) % dil == 0]
        s_n = [_lane_sum_rep(kn_ref[0, ip] * q) * scale - slope * float((i - ip) // dil) for ip in news]
        m = functools.reduce(jnp.maximum, s_n, jnp.max(s_c, axis=0))
        p_c = jnp.exp(s_c - m[None])
        p_n = [jnp.exp(s - m) for s in s_n]
        l = functools.reduce(lambda a, c: a + c, p_n, jnp.sum(p_c, axis=0))
        acc = jnp.sum(p_c * vc, axis=0)
        for p, ip in zip(p_n, news):
            acc = acc + p * vn_ref[0, ip]
        o_ref[0, i] = acc / l
        lse_ref[0, i] = m + jnp.log(l)


def dilated_attention_sample(q, k_new, v_new, cache_k, cache_v, slopes_p, win, dil):
    B, n, H, D = q.shape
    keep = cache_k.shape[2]
    band = win // dil
    assert keep >= band * dil, "window cache shorter than the attention window is not supported"
    new_spec = pl.BlockSpec((1, n, H, D), lambda i: (i, 0, 0, 0))
    c_spec = pl.BlockSpec((1, 1, keep, H, D), lambda i: (0, i, 0, 0, 0))
    shape = jax.ShapeDtypeStruct((B, n, H, D), F32)
    slope_rep = jnp.broadcast_to(slopes_p.reshape(H, 1), (H, D)).astype(F32)
    return pl.pallas_call(
        functools.partial(_dil_attn_sample_kernel, dil=dil, band=band),
        out_shape=(shape, shape),
        grid=(B,),
        in_specs=[new_spec, new_spec, new_spec, c_spec, c_spec, pl.BlockSpec((H, D), lambda i: (0, 0))],
        out_specs=(new_spec, new_spec),
        compiler_params=_params("parallel"),
        name=f"dil_attn_sample_d{dil}",
    )(q, k_new, v_new, cache_k, cache_v, slope_rep)


def _mem_attn_sample_kernel(q_ref, k_ref, v_ref, o_ref):
    n, H, D = q_ref.shape[1:]
    scale = D ** -0.5
    for h in range(H):
        qh = q_ref[0, :, h, :].astype(BF16)
        kh = k_ref[0, 0, :, h, :].astype(BF16)
        vh = v_ref[0, 0, :, h, :].astype(BF16)
        s = _dot_nt(qh, kh) * scale
        m = jnp.max(s, axis=-1, keepdims=True)
        p = jnp.exp(s - m)
        l = jnp.sum(p, axis=-1, keepdims=True)
        o_ref[0, :, h, :] = jnp.dot(p.astype(BF16), vh, preferred_element_type=F32) / l


def mem_attention_sample(q, mem_k, mem_v):
    B, n, H, D = q.shape
    M = mem_k.shape[2]
    q_spec = pl.BlockSpec((1, n, H, D), lambda i: (i, 0, 0, 0))
    m_spec = pl.BlockSpec((1, 1, M, H, D), lambda i: (0, i, 0, 0, 0))
    return pl.pallas_call(
        _mem_attn_sample_kernel,
        out_shape=jax.ShapeDtypeStruct((B, n, H, D), F32),
        grid=(B,),
        in_specs=[q_spec, m_spec, m_spec],
        out_specs=q_spec,
        compiler_params=_params("parallel"),
        name="mem_attn_sample",
    )(q, mem_k, mem_v)


def kernel(x_prompt, x_sample, mem_prompt, state_hgrn, cache_w1_k, cache_w1_v, cache_w2_k, cache_w2_v, cache_w3_k, cache_w3_v, cache_mem_k, cache_mem_v, norm_mix_g, w_in, hg_lower_bound, hg_norm_g, w_out, norm_x_g, norm_mem_g, wq_x, wk_x, wv_x, wo_x, norm_ffn_g, peer_wq, peer_subkeys, peer_u, peer_v, norm_final_g):
    b, L, D = x_prompt.shape
    db, dn, _ = x_sample.shape
    depth = w_in.shape[0]
    n_p, n_s = b * L, db * dn
    hg_heads = state_hgrn.shape[2]
    n_slots = cache_w1_k.shape[3]
    n_pat = len(DIL_PATTERNS)
    M = mem_prompt.shape[1]
    hg_cols = (0, hg_heads, 2 * hg_heads, 3 * hg_heads)
    qcol, kcol, vcol = (4 * hg_heads + s * n_pat * n_slots for s in range(3))
    lbs = jnp.cumsum(jax.nn.softmax(hg_lower_bound.astype(F32), axis=0), axis=0)
    slopes = _alibi_slopes(n_pat, n_slots)
    cache_k = (cache_w1_k, cache_w2_k, cache_w3_k)
    cache_v = (cache_w1_v, cache_w2_v, cache_w3_v)
    prompt_units = (4, 1, 1)

    h = jnp.concatenate([x_prompt.reshape(n_p, D), x_sample.reshape(n_s, D)], axis=0)
    hgp, hgs, mkp, mvp = [], [], [], []
    wkp, wvp, wks, wvs = ([[] for _ in DIL_PATTERNS] for _ in range(4))
    for l in range(depth):
        lb, ng = lbs[l].reshape(1, -1), hg_norm_g[l].reshape(1, -1)
        proj = matmul([rmsnorm(h, norm_mix_g[l])], w_in[l])

        def head_cols(col0, p, rows):
            c0 = (col0 + p * n_slots) * HEAD_DIM
            return rows[:, c0:c0 + n_slots * HEAD_DIM]

        o_hg_p, s_p = hgrn_prompt(proj, lb, ng, b, L, hg_heads, hg_cols)
        outs, lses = [], []
        for p, (win, dil) in enumerate(DIL_PATTERNS):
            o, lse = dilated_attention_prompt(proj, slopes[p], b, L, n_slots, qcol + p * n_slots, kcol + p * n_slots,
                                              vcol + p * n_slots, win, dil, prompt_units[p])
            outs.append(o)
            lses.append(lse)
            keep = min(win, L)
            for col0, dst in ((kcol, wkp), (vcol, wvp)):
                rows = head_cols(col0, p, proj[:n_p]).reshape(b, L, n_slots, HEAD_DIM)
                dst[p].append(rows[:, L - keep:])
        o_at_p = merge_patterns(outs, lses)
        hgp.append(s_p)

        o_hg_s, s_s = hgrn_sample(proj, n_p, state_hgrn[l:l + 1], lb, ng, dn, hg_cols)
        hgs.append(s_s[0])
        proj_s = proj[n_p:]
        outs, lses = [], []
        for p, (win, dil) in enumerate(DIL_PATTERNS):
            q_s, k_s, v_s = (head_cols(c, p, proj_s).reshape(db, dn, n_slots, HEAD_DIM) for c in (qcol, kcol, vcol))
            ck, cv = cache_k[p][l:l + 1], cache_v[p][l:l + 1]
            o, lse = dilated_attention_sample(q_s, k_s, v_s, ck, cv, slopes[p], win, dil)
            outs.append(o.reshape(n_s, n_slots * HEAD_DIM))
            lses.append(lse.reshape(n_s, n_slots * HEAD_DIM))
            wks[p].append(cache_shift(ck, k_s)[0])
            wvs[p].append(cache_shift(cv, v_s)[0])
        o_at_s = merge_patterns(outs, lses)

        mix_hg = jnp.concatenate([o_hg_p, o_hg_s], axis=0)
        mix_at = jnp.concatenate([o_at_p, o_at_s], axis=0)
        h = matmul([mix_hg, mix_at], w_out[l], res=h)

        nm = rmsnorm(mem_prompt.reshape(b * M, D), norm_mem_g[l])
        mk, mv = matmul([nm], wk_x[l]), matmul([nm], wv_x[l])
        mkp.append(mk.reshape(b, M, MEM_HEADS, D // MEM_HEADS))
        mvp.append(mv.reshape(b, M, MEM_HEADS, D // MEM_HEADS))
        qx = matmul([rmsnorm(h, norm_x_g[l])], wq_x[l])
        a_p = mem_attention_prompt(qx, mk, mv, b, L, M, MEM_HEADS)
        a_s = mem_attention_sample(qx[n_p:].reshape(db, dn, MEM_HEADS, D // MEM_HEADS),
                                   cache_mem_k[l:l + 1], cache_mem_v[l:l + 1])
        att = jnp.concatenate([a_p, a_s.reshape(n_s, D).astype(BF16)], axis=0)
        h = matmul([att], wo_x[l], res=h)

        n3 = rmsnorm(h, norm_ffn_g[l])
        gates = peer_router(matmul([n3], peer_wq[l]), peer_subkeys[l])
        ffn = peer_experts(n3, gates, peer_u[l], peer_v[l])
        if l + 1 < depth:
            h = add_rmsnorm(h, ffn, None)
    y = add_rmsnorm(h, ffn, norm_final_g)
    y_prompt, y_sample = y[:n_p].reshape(b, L, D), y[n_p:].reshape(db, dn, D)
    st = lambda xs: jnp.stack(xs)
    return (y_prompt, y_sample, st(hgp), st(wkp[0]), st(wvp[0]), st(wkp[1]), st(wvp[1]), st(wkp[2]), st(wvp[2]),
            st(mkp), st(mvp), st(hgs), st(wks[0]), st(wvs[0]), st(wks[1]), st(wvs[1]), st(wks[2]), st(wvs[2]))
```

```python
import functools
import math

import jax
import jax.numpy as jnp
from jax import lax
from jax.experimental import pallas as pl
from jax.experimental.pallas import tpu as pltpu

F32 = jnp.float32
BF16 = jnp.bfloat16
NEG_INF = -1e30
RMS_EPS = 1e-6
HEAD_DIM = 128
HG_CHUNK = 16
DIL_PATTERNS = ((128, 1), (512, 4), (2048, 16))
ALIBI_MAX = 8.0
MEM_HEADS = 4
PK_HEADS = 8
N_KEYS = 128
PK_TOPK = 16

VMEM_LIMIT_BYTES = 56 * 1024 * 1024


def _params(*sem):
    return pltpu.CompilerParams(dimension_semantics=sem, vmem_limit_bytes=VMEM_LIMIT_BYTES)


def _row_tile(n, target):
    best = None
    for t in range(8, min(n, target) + 1, 8):
        if n % t == 0:
            best = t
    assert best is not None, (n, target)
    return best


def _rmsnorm_kernel(x_ref, g_ref, o_ref):
    x = x_ref[...]
    ms = jnp.mean(x * x, axis=-1, keepdims=True)
    o_ref[...] = (x * lax.rsqrt(ms + RMS_EPS) * g_ref[...]).astype(o_ref.dtype)


def rmsnorm(x, g, out_dtype=BF16, tm=512):
    T, D = x.shape
    tm = _row_tile(T, tm)
    return pl.pallas_call(
        _rmsnorm_kernel,
        out_shape=jax.ShapeDtypeStruct((T, D), out_dtype),
        grid=(T // tm,),
        in_specs=[pl.BlockSpec((tm, D), lambda i: (i, 0)), pl.BlockSpec((1, D), lambda i: (0, 0))],
        out_specs=pl.BlockSpec((tm, D), lambda i: (i, 0)),
        compiler_params=_params("parallel"),
        name="rmsnorm",
    )(x, g.reshape(1, D).astype(F32))


def _add_rmsnorm_kernel(x_ref, y_ref, g_ref, o_ref, *, normalize):
    x = x_ref[...] + y_ref[...]
    if normalize:
        ms = jnp.mean(x * x, axis=-1, keepdims=True)
        x = x * lax.rsqrt(ms + RMS_EPS) * g_ref[...]
    o_ref[...] = x.astype(o_ref.dtype)


def add_rmsnorm(x, y, g, tm=256):
    T, D = x.shape
    tm = _row_tile(T, tm)
    gain = jnp.ones((1, D), F32) if g is None else g.reshape(1, D).astype(F32)
    return pl.pallas_call(
        functools.partial(_add_rmsnorm_kernel, normalize=g is not None),
        out_shape=jax.ShapeDtypeStruct((T, D), F32),
        grid=(T // tm,),
        in_specs=[pl.BlockSpec((tm, D), lambda i: (i, 0)), pl.BlockSpec((tm, D), lambda i: (i, 0)),
                  pl.BlockSpec((1, D), lambda i: (0, 0))],
        out_specs=pl.BlockSpec((tm, D), lambda i: (i, 0)),
        compiler_params=_params("parallel"),
        name="add_rmsnorm",
    )(x, y, gain)


def _mm_kernel(*refs, k_splits, has_res):
    n_a = len(k_splits)
    a_refs, w_ref = refs[:n_a], refs[n_a]
    res_ref = refs[n_a + 1] if has_res else None
    o_ref = refs[-1]
    acc, off = None, 0
    for a_ref, kk in zip(a_refs, k_splits):
        part = jnp.dot(a_ref[...], w_ref[off:off + kk, :].astype(BF16), preferred_element_type=F32)
        acc = part if acc is None else acc + part
        off += kk
    if has_res:
        acc = acc + res_ref[...]
    o_ref[...] = acc.astype(o_ref.dtype)


def matmul(a_list, w, res=None, out_dtype=F32, tm=1088, tn=512):
    M = a_list[0].shape[0]
    k_splits = tuple(a.shape[1] for a in a_list)
    K, N = w.shape
    assert sum(k_splits) == K
    tm, tn = _row_tile(M, tm), min(tn, N)
    assert N % tn == 0
    in_specs = [pl.BlockSpec((tm, kk), lambda i, j: (i, 0)) for kk in k_splits]
    in_specs.append(pl.BlockSpec((K, tn), lambda i, j: (0, j)))
    args = list(a_list) + [w]
    if res is not None:
        in_specs.append(pl.BlockSpec((tm, tn), lambda i, j: (i, j)))
        args.append(res)
    return pl.pallas_call(
        functools.partial(_mm_kernel, k_splits=k_splits, has_res=res is not None),
        out_shape=jax.ShapeDtypeStruct((M, N), out_dtype),
        grid=(M // tm, N // tn),
        in_specs=in_specs,
        out_specs=pl.BlockSpec((tm, tn), lambda i, j: (i, j)),
        compiler_params=_params("parallel", "arbitrary"),
        name="matmul",
    )(*args)


def _silu(x):
    return x * jax.nn.sigmoid(x)


def _split3(x):
    hi = x.astype(BF16)
    r1 = x - hi.astype(F32)
    mid = r1.astype(BF16)
    lo = (r1 - mid.astype(F32)).astype(BF16)
    return hi, mid, lo


def _rows_bcast(x, row_ids, reps):
    W = x.shape[1]
    parts = []
    for r in row_ids:
        row = jnp.zeros((1, W), x.dtype) if r < 0 else x[r:r + 1, :]
        parts.append(jnp.broadcast_to(row, (reps, W)))
    return parts[0] if len(parts) == 1 else jnp.concatenate(parts, axis=0)


def _dot_nt(a, b):
    return lax.dot_general(a, b, (((1,), (1,)), ((), ())), preferred_element_type=F32)


def _dot_tn(a, b):
    return lax.dot_general(a, b, (((0,), (0,)), ((), ())), preferred_element_type=F32)


def _hgrn_chunk(q, k, g, v, st, leaf):
    C = q.shape[0]
    row = lax.broadcasted_iota(jnp.int32, (C, q.shape[1]), 0)
    it = lax.broadcasted_iota(jnp.int32, (C, C), 0)
    js = lax.broadcasted_iota(jnp.int32, (C, C), 1)
    tri = jnp.where(it >= js, 1.0, 0.0).astype(BF16)
    g_hi, g_mid, g_lo = _split3(g)
    cum = (jnp.dot(tri, g_hi, preferred_element_type=F32) + jnp.dot(tri, g_mid, preferred_element_type=F32)
           + jnp.dot(tri, g_lo, preferred_element_type=F32))
    last = cum[C - 1:C, :]

    a = jnp.zeros((C, C), F32)
    bs = C
    while bs > leaf:
        half = bs // 2
        upper = (row % bs) >= half
        cmid = _rows_bcast(cum, [b0 + half - 1 for b0 in range(0, C, bs)], bs)
        e = jnp.exp(jnp.where(upper, cum - cmid, cmid - cum))
        q_l = jnp.where(upper, q * e, 0.0).astype(BF16)
        k_l = jnp.where(upper, 0.0, k * e).astype(BF16)
        a = a + jnp.where((it // bs) == (js // bs), _dot_nt(q_l, k_l), 0.0)
        bs = half
    cstart = _rows_bcast(cum, [b0 - 1 for b0 in range(0, C, leaf)], leaf)
    q_f = (q * jnp.exp(cum - cstart)).astype(BF16)
    k_f = (k * jnp.exp(cstart - cum)).astype(BF16)
    a = a + jnp.where(((it // leaf) == (js // leaf)) & (it >= js), _dot_nt(q_f, k_f), 0.0)

    vb = v.astype(BF16)
    q_g = (q * jnp.exp(cum)).astype(BF16)
    o = jnp.dot(a.astype(BF16), vb, preferred_element_type=F32) + _dot_nt(q_g, st.astype(BF16))
    k_e = (k * jnp.exp(last - cum)).astype(BF16)
    st_new = st * jnp.exp(last) + _dot_tn(vb, k_e)
    return o, st_new


def _hgrn_gates(hq, hf, lb):
    q = _silu(hq)
    f = lb + (1.0 - lb) * jax.nn.sigmoid(hf)
    return q, 1.0 - f, jnp.log(f)


def _hgrn_out(o, hgate, ng):
    ms = jnp.mean(o * o, axis=-1, keepdims=True)
    return o * lax.rsqrt(ms + RMS_EPS) * ng * _silu(hgate)


def _hgrn_prompt_kernel(hq_ref, hf_ref, hi_ref, hg_ref, lb_ref, ng_ref, o_ref, s_ref, st_scr, *, chunk, leaf):
    c = pl.program_id(2)

    @pl.when(c == 0)
    def _():
        st_scr[...] = jnp.zeros_like(st_scr)

    rows = hq_ref.shape[0]
    lb, ng = lb_ref[...], ng_ref[...]
    for s0 in range(0, rows, chunk):
        sl = slice(s0, s0 + chunk)
        q, k, g = _hgrn_gates(hq_ref[sl, :], hf_ref[sl, :], lb)
        o, st_new = _hgrn_chunk(q, k, g, hi_ref[sl, :], st_scr[...], leaf)
        st_scr[...] = st_new
        o_ref[sl, :] = _hgrn_out(o, hg_ref[sl, :], ng).astype(o_ref.dtype)

    @pl.when(c == pl.num_programs(2) - 1)
    def _():
        s_ref[0, 0] = st_scr[...].T


def hgrn_prompt(proj, lb, ng, b, L, n_heads, col_blocks, rows_per_step=512, chunk=128, leaf=HG_CHUNK):
    rows = min(rows_per_step, L)
    assert L % rows == 0 and rows % chunk == 0
    nc = L // rows

    def in_spec(cb):
        return pl.BlockSpec((rows, HEAD_DIM), lambda bi, h, c: (bi * nc + c, cb + h))

    return pl.pallas_call(
        functools.partial(_hgrn_prompt_kernel, chunk=chunk, leaf=leaf),
        out_shape=(jax.ShapeDtypeStruct((b * L, n_heads * HEAD_DIM), BF16),
                   jax.ShapeDtypeStruct((b, n_heads, HEAD_DIM, HEAD_DIM), F32)),
        grid=(b, n_heads, nc),
        in_specs=[in_spec(cb) for cb in col_blocks] + [
            pl.BlockSpec((1, HEAD_DIM), lambda bi, h, c: (0, h)),
            pl.BlockSpec((1, HEAD_DIM), lambda bi, h, c: (0, 0))],
        out_specs=(pl.BlockSpec((rows, HEAD_DIM), lambda bi, h, c: (bi * nc + c, h)),
                   pl.BlockSpec((1, 1, HEAD_DIM, HEAD_DIM), lambda bi, h, c: (bi, h, 0, 0))),
        scratch_shapes=[pltpu.VMEM((HEAD_DIM, HEAD_DIM), F32)],
        compiler_params=_params("parallel", "parallel", "arbitrary"),
        name="hgrn_prompt",
    )(proj, proj, proj, proj, lb, ng)


def _alibi_slopes(n_pat, n_slots):
    k = jnp.arange(1, n_pat * n_slots + 1, dtype=F32)
    return (2.0 ** (-ALIBI_MAX * k / (n_pat * n_slots))).reshape(n_pat, n_slots)


def _strided_rows(ref, start, size, stride):
    if stride == 1:
        return ref[pl.ds(start, size), :]
    return ref[pl.ds(start, size, stride=stride), :]


def _band_softmax(q, kk, vv, bias, valid, scale):
    s = _dot_nt(q.astype(BF16), kk.astype(BF16)) * scale - bias
    s = jnp.where(valid, s, NEG_INF)
    m = jnp.max(s, axis=-1, keepdims=True)
    p = jnp.exp(s - m)
    l = jnp.sum(p, axis=-1, keepdims=True)
    o = jnp.dot(p.astype(BF16), vv.astype(BF16), preferred_element_type=F32) / l
    return o, m + jnp.log(l)


def _dil_attn_kernel(q_ref, kp_ref, kc_ref, vp_ref, vc_ref, slope_ref, o_ref, lse_ref, *, dil, band, units):
    first = pl.program_id(2) == 0
    span = band * dil
    scale = HEAD_DIM ** -0.5
    qi = lax.broadcasted_iota(jnp.int32, (band, 2 * band), 0)
    kj = lax.broadcasted_iota(jnp.int32, (band, 2 * band), 1)
    off = qi + band - kj
    in_band = (off >= 0) & (off <= band)
    bias = slope_ref[0] * float(dil) * off.astype(F32)
    for u in range(units):
        for r in range(dil):
            base = u * span + r
            q = _strided_rows(q_ref, base, band, dil)
            if u == 0:
                kp, vp = _strided_rows(kp_ref, r, band, dil), _strided_rows(vp_ref, r, band, dil)
                valid = in_band & (jnp.logical_not(first) | (kj >= band))
            else:
                kp, vp = _strided_rows(kc_ref, base - span, band, dil), _strided_rows(vc_ref, base - span, band, dil)
                valid = in_band
            kk = jnp.concatenate([kp, _strided_rows(kc_ref, base, band, dil)], axis=0)
            vv = jnp.concatenate([vp, _strided_rows(vc_ref, base, band, dil)], axis=0)
            o, lse = _band_softmax(q, kk, vv, bias, valid, scale)
            lse_b = jnp.broadcast_to(lse, (band, HEAD_DIM))
            if dil == 1:
                o_ref[pl.ds(base, band), :] = o
                lse_ref[pl.ds(base, band), :] = lse_b
            else:
                o_ref[pl.ds(base, band, stride=dil), :] = o
                lse_ref[pl.ds(base, band, stride=dil), :] = lse_b


def dilated_attention_prompt(proj, slopes_p, b, L, n_slots, qcol, kcol, vcol, win, dil, units):
    band = win // dil
    span = band * dil
    rows = span * units
    assert L % rows == 0
    nblk = L // rows
    cur = lambda c0: pl.BlockSpec((rows, HEAD_DIM), lambda bi, h, i: (bi * nblk + i, c0 + h))
    prev = lambda c0: pl.BlockSpec(
        (span, HEAD_DIM), lambda bi, h, i: (bi * nblk * units + jnp.maximum(i * units - 1, 0), c0 + h))
    out = pl.BlockSpec((rows, HEAD_DIM), lambda bi, h, i: (bi * nblk + i, h))
    shape = jax.ShapeDtypeStruct((b * L, n_slots * HEAD_DIM), F32)
    slope_rep = jnp.broadcast_to(slopes_p.reshape(n_slots, 1, 1), (n_slots, 1, 2 * band)).astype(F32)
    return pl.pallas_call(
        functools.partial(_dil_attn_kernel, dil=dil, band=band, units=units),
        out_shape=(shape, shape),
        grid=(b, n_slots, nblk),
        in_specs=[cur(qcol), prev(kcol), cur(kcol), prev(vcol), cur(vcol),
                  pl.BlockSpec((1, 1, 2 * band), lambda bi, h, i: (h, 0, 0))],
        out_specs=(out, out),
        compiler_params=_params("parallel", "parallel", "arbitrary"),
        name=f"dil_attn_prompt_d{dil}",
    )(proj, proj, proj, proj, proj, slope_rep)


def _merge_kernel(*refs):
    n = (len(refs) - 1) // 2
    o_refs, l_refs, out_ref = refs[:n], refs[n:2 * n], refs[-1]
    ls = [r[...] for r in l_refs]
    m = functools.reduce(jnp.maximum, ls)
    ws = [jnp.exp(l - m) for l in ls]
    num = functools.reduce(lambda a, c: a + c, [w * r[...] for w, r in zip(ws, o_refs)])
    den = functools.reduce(lambda a, c: a + c, ws)
    out_ref[...] = (num / den).astype(out_ref.dtype)


def merge_patterns(outs, lses, tm=512):
    T, W = outs[0].shape
    tm = _row_tile(T, tm)
    spec = pl.BlockSpec((tm, W), lambda i: (i, 0))
    return pl.pallas_call(
        _merge_kernel,
        out_shape=jax.ShapeDtypeStruct((T, W), BF16),
        grid=(T // tm,),
        in_specs=[spec] * (2 * len(outs)),
        out_specs=spec,
        compiler_params=_params("parallel"),
        name="merge_patterns",
    )(*outs, *lses)


def _mem_attn_kernel(q_ref, k_ref, v_ref, o_ref):
    scale = q_ref.shape[-1] ** -0.5
    s = _dot_nt(q_ref[...].astype(BF16), k_ref[...].astype(BF16)) * scale
    m = jnp.max(s, axis=-1, keepdims=True)
    p = jnp.exp(s - m)
    l = jnp.sum(p, axis=-1, keepdims=True)
    o = jnp.dot(p.astype(BF16), v_ref[...].astype(BF16), preferred_element_type=F32) / l
    o_ref[...] = o.astype(o_ref.dtype)


def mem_attention_prompt(q, mk, mv, b, L, M, n_heads, tq=1024):
    D = mk.shape[1]
    hd = D // n_heads
    tq = _row_tile(L, tq)
    nq = L // tq
    return pl.pallas_call(
        _mem_attn_kernel,
        out_shape=jax.ShapeDtypeStruct((b * L, D), BF16),
        grid=(b, n_heads, nq),
        in_specs=[pl.BlockSpec((tq, hd), lambda bi, h, i: (bi * nq + i, h)),
                  pl.BlockSpec((M, hd), lambda bi, h, i: (bi, h)),
                  pl.BlockSpec((M, hd), lambda bi, h, i: (bi, h))],
        out_specs=pl.BlockSpec((tq, hd), lambda bi, h, i: (bi * nq + i, h)),
        compiler_params=_params("parallel", "parallel", "arbitrary"),
        name="mem_attn_prompt",
    )(q, mk, mv)


def _dot_nt_x3(a, b):
    a_hi, b_hi = a.astype(BF16), b.astype(BF16)
    a_lo, b_lo = (a - a_hi.astype(F32)).astype(BF16), (b - b_hi.astype(F32)).astype(BF16)
    return _dot_nt(a_hi, b_hi) + _dot_nt(a_hi, b_lo) + _dot_nt(a_lo, b_hi)


def _topk_rows(x, k, payload=None):
    n = x.shape[0]
    iota = lax.broadcasted_iota(jnp.int32, x.shape, 0).astype(F32)
    vals, outs = [], []
    for _ in range(k):
        m = jnp.max(x, axis=0, keepdims=True)
        pos = jnp.min(jnp.where(x == m, iota, float(n)), axis=0, keepdims=True)
        sel = iota == pos
        vals.append(m)
        outs.append(pos if payload is None else jnp.max(jnp.where(sel, payload, -1.0), axis=0, keepdims=True))
        x = jnp.where(sel, -jnp.inf, x)
    return jnp.concatenate(vals, axis=0), jnp.concatenate(outs, axis=0)


def _candidates(v1, i1, v2, i2, kk):
    vals, idxs, a = [], [], 0
    while a < kk and kk // (a + 1) >= 2:
        nb = min(kk, -(-(kk // (a + 1)) // 8) * 8)
        vals.append(v1[a:a + 1, :] + v2[:nb, :])
        idxs.append(i1[a:a + 1, :] * float(N_KEYS) + i2[:nb, :])
        a += 1
    if a < kk:
        vals.append(v1[a:, :] + v2[0:1, :])
        idxs.append(i1[a:, :] * float(N_KEYS) + i2[0:1, :])
    return jnp.concatenate(vals, axis=0), jnp.concatenate(idxs, axis=0)


_STAGE_PITCH = N_KEYS + 8


def _router_kernel(qp_ref, sk_ref, g_ref, ii_scr, ij_scr, w_scr, iit_scr, ijt_scr, wt_scr, stage_scr):
    tb = qp_ref.shape[0]
    kk = PK_TOPK
    for h in range(PK_HEADS):
        tops = []
        for c in range(2):
            col = (2 * h + c) * HEAD_DIM
            sc = _dot_nt_x3(sk_ref[2 * h + c], qp_ref[:, col:col + HEAD_DIM])
            tops.append(_topk_rows(sc, kk))
        (v1, i1), (v2, i2) = tops
        cand, cidx = _candidates(v1, i1, v2, i2, kk)
        best, eidx = _topk_rows(cand, kk, payload=cidx)
        e = jnp.exp(best - best[0:1, :])
        gate = e / jnp.sum(e, axis=0, keepdims=True)
        ei = jnp.floor(eidx * (1.0 / N_KEYS))
        ii_scr[h * kk:(h + 1) * kk, :] = ei
        ij_scr[h * kk:(h + 1) * kk, :] = eidx - ei * float(N_KEYS)
        w_scr[h * kk:(h + 1) * kk, :] = gate
    iit_scr[...] = ii_scr[...].T
    ijt_scr[...] = ij_scr[...].T
    wt_scr[...] = w_scr[...].T
    sub = lax.broadcasted_iota(jnp.int32, (N_KEYS, PK_HEADS * kk), 0).astype(F32)

    def per_token(t, carry):
        a = jnp.where(sub == iit_scr[pl.ds(t, 1), :], wt_scr[pl.ds(t, 1), :], 0.0).astype(BF16)
        bsel = jnp.where(sub == ijt_scr[pl.ds(t, 1), :], 1.0, 0.0).astype(BF16)
        row0 = pl.multiple_of(t * _STAGE_PITCH, 8)
        stage_scr[pl.ds(row0, N_KEYS), :] = _dot_nt(a, bsel)
        return carry

    lax.fori_loop(0, tb, per_token, 0, unroll=8)
    for i in range(N_KEYS):
        g_ref[:, i * N_KEYS:(i + 1) * N_KEYS] = stage_scr[pl.ds(i, tb, stride=_STAGE_PITCH), :].astype(g_ref.dtype)


def peer_router(qp, subkeys, tb=128):
    T = qp.shape[0]
    assert T % tb == 0 and tb == N_KEYS
    sk = subkeys.reshape(PK_HEADS * 2, N_KEYS, HEAD_DIM)
    slots = PK_HEADS * PK_TOPK
    return pl.pallas_call(
        _router_kernel,
        out_shape=jax.ShapeDtypeStruct((T, N_KEYS * N_KEYS), BF16),
        grid=(T // tb,),
        in_specs=[pl.BlockSpec((tb, qp.shape[1]), lambda i: (i, 0)),
                  pl.BlockSpec(sk.shape, lambda i: (0, 0, 0))],
        out_specs=pl.BlockSpec((tb, N_KEYS * N_KEYS), lambda i: (i, 0)),
        scratch_shapes=[pltpu.VMEM((slots, tb), F32)] * 3 + [pltpu.VMEM((tb, slots), F32)] * 3
        + [pltpu.VMEM((tb * _STAGE_PITCH, N_KEYS), F32)],
        compiler_params=_params("parallel"),
        name="peer_router",
    )(qp, sk)


def _gelu(x):
    return 0.5 * x * (1.0 + lax.erf(x * (2.0 ** -0.5)))


def _cast_kernel(x_ref, o_ref):
    o_ref[...] = x_ref[...].astype(o_ref.dtype)


def cast(x, dtype, tm=512):
    R, C = x.shape
    tm = _row_tile(R, tm)
    spec = pl.BlockSpec((tm, C), lambda i: (i, 0))
    return pl.pallas_call(
        _cast_kernel, out_shape=jax.ShapeDtypeStruct((R, C), dtype), grid=(R // tm,), in_specs=[spec],
        out_specs=spec, compiler_params=_params("parallel"), name="cast",
    )(x)


def _peer_kernel(n_ref, g_ref, u_ref, v_ref, o_ref, hid_a, hid_b):
    j = pl.program_id(1)

    @pl.when(j == 0)
    def _():
        hid_b[...] = jnp.zeros_like(hid_b)
        o_ref[...] = jnp.zeros_like(o_ref)

    def step(prev_scr, next_scr):
        part = jnp.dot(prev_scr[...], v_ref[...], preferred_element_type=F32)
        xu = _dot_nt(n_ref[...], u_ref[...])
        next_scr[...] = (g_ref[...].astype(F32) * _gelu(xu)).astype(BF16)
        o_ref[...] += part

    @pl.when(j % 2 == 0)
    def _():
        step(hid_b, hid_a)

    @pl.when(j % 2 == 1)
    def _():
        step(hid_a, hid_b)


def peer_experts(n, gates, u, v, tm=544, te=256):
    T, D = n.shape
    E = u.shape[0]
    tm = _row_tile(T, tm)
    assert E % te == 0
    ne = E // te
    return pl.pallas_call(
        _peer_kernel,
        out_shape=jax.ShapeDtypeStruct((T, D), F32),
        grid=(T // tm, ne + 1),
        in_specs=[pl.BlockSpec((tm, D), lambda i, j: (i, 0)),
                  pl.BlockSpec((tm, te), lambda i, j: (i, jnp.minimum(j, ne - 1))),
                  pl.BlockSpec((te, D), lambda i, j: (jnp.minimum(j, ne - 1), 0)),
                  pl.BlockSpec((te, D), lambda i, j: (jnp.maximum(j - 1, 0), 0))],
        out_specs=pl.BlockSpec((tm, D), lambda i, j: (i, 0)),
        scratch_shapes=[pltpu.VMEM((tm, te), BF16)] * 2,
        compiler_params=_params("parallel", "arbitrary"),
        name="peer_experts",
    )(n, gates, u, v)


def _hgrn_sample_kernel(hq_ref, hf_ref, hi_ref, hg_ref, lb_ref, ng_ref, s_ref, o_ref, so_ref, *, n_tok):
    rows, width = hq_ref.shape
    n_seq, n_head = rows // n_tok, width // HEAD_DIM
    q, k, g = _hgrn_gates(hq_ref[...], hf_ref[...], lb_ref[...])
    t = lax.broadcasted_iota(jnp.int32, (rows, width), 0) % n_tok
    cum, sh = g, 1
    while sh < n_tok:
        cum = cum + jnp.where(t >= sh, pltpu.roll(cum, sh, axis=0), 0.0)
        sh *= 2
    last = _rows_bcast(cum, [s * n_tok + n_tok - 1 for s in range(n_seq)], n_tok)
    q_in = q * jnp.exp(cum)
    k_in = k * jnp.exp(-cum)
    k_out = k * jnp.exp(last - cum)
    dec_t = jnp.exp(last).T
    v = hi_ref[...]
    it = lax.broadcasted_iota(jnp.int32, (n_tok, n_tok), 0)
    js = lax.broadcasted_iota(jnp.int32, (n_tok, n_tok), 1)
    ng = ng_ref[...]
    for s in range(n_seq):
        r = slice(s * n_tok, (s + 1) * n_tok)
        for h in range(n_head):
            c = slice(h * HEAD_DIM, (h + 1) * HEAD_DIM)
            st = s_ref[0, s, h]
            qh, vh = q_in[r, c].astype(BF16), v[r, c].astype(BF16)
            a = jnp.where(it >= js, _dot_nt(qh, k_in[r, c].astype(BF16)), 0.0)
            o = jnp.dot(a.astype(BF16), vh, preferred_element_type=F32)
            o = o + jnp.dot(qh, st.astype(BF16), preferred_element_type=F32)
            dec = dec_t[c, s * n_tok:s * n_tok + 1]
            so_ref[0, s, h] = st * dec + _dot_tn(k_out[r, c].astype(BF16), vh)
            o_ref[r, c] = _hgrn_out(o, hg_ref[r, c], ng).astype(o_ref.dtype)


def hgrn_sample(proj, row0, state, lb, ng, n_tok, col_blocks, seqs_per_step=4, heads_per_step=4):
    _, n_seq, n_heads, _, _ = state.shape
    rows, width = seqs_per_step * n_tok, heads_per_step * HEAD_DIM
    assert n_seq % seqs_per_step == 0 and n_heads % heads_per_step == 0 and row0 % rows == 0
    r0, nhb = row0 // rows, n_heads // heads_per_step

    def in_spec(cb):
        return pl.BlockSpec((rows, width), lambda i, j: (r0 + i, cb // heads_per_step + j))

    st_spec = pl.BlockSpec((1, seqs_per_step, heads_per_step, HEAD_DIM, HEAD_DIM), lambda i, j: (0, i, j, 0, 0))
    return pl.pallas_call(
        functools.partial(_hgrn_sample_kernel, n_tok=n_tok),
        out_shape=(jax.ShapeDtypeStruct((n_seq * n_tok, n_heads * HEAD_DIM), BF16),
                   jax.ShapeDtypeStruct(state.shape, F32)),
        grid=(n_seq // seqs_per_step, nhb),
        in_specs=[in_spec(cb) for cb in col_blocks] + [
            pl.BlockSpec((1, width), lambda i, j: (0, j)),
            pl.BlockSpec((1, HEAD_DIM), lambda i, j: (0, 0)),
            st_spec],
        out_specs=(pl.BlockSpec((rows, width), lambda i, j: (i, j)), st_spec),
        compiler_params=_params("parallel", "parallel"),
        name="hgrn_sample",
    )(proj, proj, proj, proj, lb, ng, state)


def _cache_shift_kernel(c_ref, new_ref, o_ref):
    keep, n_new = c_ref.shape[2], new_ref.shape[1]
    o_ref[0, 0, 0:keep - n_new] = c_ref[0, 0, n_new:keep]
    o_ref[0, 0, keep - n_new:keep] = new_ref[0]


def cache_shift(cache, new):
    _, B, keep, H, D = cache.shape
    n = new.shape[1]
    if n >= keep:
        return new[None, :, n - keep:]
    blk = pl.BlockSpec((1, 1, keep, H, D), lambda i: (0, i, 0, 0, 0))
    return pl.pallas_call(
        _cache_shift_kernel,
        out_shape=jax.ShapeDtypeStruct(cache.shape, cache.dtype),
        grid=(B,),
        in_specs=[blk, pl.BlockSpec((1, n, H, D), lambda i: (i, 0, 0, 0))],
        out_specs=blk,
        compiler_params=_params("parallel"),
        name="cache_shift",
    )(cache, new)


def _lane_sum_rep(x):
    ones = jnp.ones((x.shape[1], HEAD_DIM), BF16)
    hi = x.astype(BF16)
    lo = (x - hi.astype(F32)).astype(BF16)
    return jnp.dot(hi, ones, preferred_element_type=F32) + jnp.dot(lo, ones, preferred_element_type=F32)


def _dil_attn_sample_kernel(q_ref, kn_ref, vn_ref, ck_ref, cv_ref, slope_ref, o_ref, lse_ref, *, dil, band):
    n, H, D = q_ref.shape[1:]
    keep = ck_ref.shape[2]
    scale = D ** -0.5
    slope = slope_ref[...] * float(dil)
    jm = lax.broadcasted_iota(jnp.int32, (band, H, D), 0)
    for i in range(n):
        q = q_ref[0, i]
        if dil >= n:
            start = keep + i - band * dil
            kc = ck_ref[0, 0, pl.ds(start, band, stride=dil)]
            vc = cv_ref[0, 0, pl.ds(start, band, stride=dil)]
            jc = band - jm
            ok_c = None
        else:
            assert dil == 1
            kc = ck_ref[0, 0, pl.ds(keep - band, band)]
            vc = cv_ref[0, 0, pl.ds(keep - band, band)]
            jc = band + i - jm
            ok_c = jc <= band
        s_c = _lane_sum_rep((kc * q[None]).reshape(band * H, D)).reshape(band, H, D) * scale
        s_c = s_c - slope[None] * jc.astype(F32)
        if ok_c is not None:
            s_c = jnp.where(ok_c, s_c, NEG_INF)
        news = [ip for ip in range(i + 1) if (i - ip) % dil == 0]
        s_n = [_lane_sum_rep(kn_ref[0, ip] * q) * scale - slope * float((i - ip) // dil) for ip in news]
        m = functools.reduce(jnp.maximum, s_n, jnp.max(s_c, axis=0))
        p_c = jnp.exp(s_c - m[None])
        p_n = [jnp.exp(s - m) for s in s_n]
        l = functools.reduce(lambda a, c: a + c, p_n, jnp.sum(p_c, axis=0))
        acc = jnp.sum(p_c * vc, axis=0)
        for p, ip in zip(p_n, news):
            acc = acc + p * vn_ref[0, ip]
        o_ref[0, i] = acc / l
        lse_ref[0, i] = m + jnp.log(l)


def dilated_attention_sample(q, k_new, v_new, cache_k, cache_v, slopes_p, win, dil):
    B, n, H, D = q.shape
    keep = cache_k.shape[2]
    band = win // dil
    assert keep >= band * dil, "window cache shorter than the attention window is not supported"
    new_spec = pl.BlockSpec((1, n, H, D), lambda i: (i, 0, 0, 0))
    c_spec = pl.BlockSpec((1, 1, keep, H, D), lambda i: (0, i, 0, 0, 0))
    shape = jax.ShapeDtypeStruct((B, n, H, D), F32)
    slope_rep = jnp.broadcast_to(slopes_p.reshape(H, 1), (H, D)).astype(F32)
    return pl.pallas_call(
        functools.partial(_dil_attn_sample_kernel, dil=dil, band=band),
        out_shape=(shape, shape),
        grid=(B,),
        in_specs=[new_spec, new_spec, new_spec, c_spec, c_spec, pl.BlockSpec((H, D), lambda i: (0, 0))],
        out_specs=(new_spec, new_spec),
        compiler_params=_params("parallel"),
        name=f"dil_attn_sample_d{dil}",
    )(q, k_new, v_new, cache_k, cache_v, slope_rep)


def _mem_attn_sample_kernel(q_ref, k_ref, v_ref, o_ref):
    n, H, D = q_ref.shape[1:]
    M = k_ref.shape[2]
    scale = D ** -0.5
    k2 = k_ref[0, 0].reshape(M * H, D).astype(BF16)
    v2 = v_ref[0, 0].reshape(M * H, D).astype(BF16)
    q2 = q_ref[0].reshape(n * H, D).astype(BF16)
    s = _dot_nt(q2, k2) * scale
    qh = lax.broadcasted_iota(jnp.int32, s.shape, 0) % H
    kh = lax.broadcasted_iota(jnp.int32, s.shape, 1) % H
    s = jnp.where(qh == kh, s, NEG_INF)
    m = jnp.max(s, axis=-1, keepdims=True)
    p = jnp.exp(s - m)
    l = jnp.sum(p, axis=-1, keepdims=True)
    o = jnp.dot(p.astype(BF16), v2, preferred_element_type=F32) / l
    o_ref[0] = o.reshape(n, H, D)


def mem_attention_sample(q, mem_k, mem_v):
    B, n, H, D = q.shape
    M = mem_k.shape[2]
    q_spec = pl.BlockSpec((1, n, H, D), lambda i: (i, 0, 0, 0))
    m_spec = pl.BlockSpec((1, 1, M, H, D), lambda i: (0, i, 0, 0, 0))
    return pl.pallas_call(
        _mem_attn_sample_kernel,
        out_shape=jax.ShapeDtypeStruct((B, n, H, D), F32),
        grid=(B,),
        in_specs=[q_spec, m_spec, m_spec],
        out_specs=q_spec,
        compiler_params=_params("parallel"),
        name="mem_attn_sample",
    )(q, mem_k, mem_v)


def kernel(x_prompt, x_sample, mem_prompt, state_hgrn, cache_w1_k, cache_w1_v, cache_w2_k, cache_w2_v, cache_w3_k, cache_w3_v, cache_mem_k, cache_mem_v, norm_mix_g, w_in, hg_lower_bound, hg_norm_g, w_out, norm_x_g, norm_mem_g, wq_x, wk_x, wv_x, wo_x, norm_ffn_g, peer_wq, peer_subkeys, peer_u, peer_v, norm_final_g):
    b, L, D = x_prompt.shape
    db, dn, _ = x_sample.shape
    depth = w_in.shape[0]
    n_p, n_s = b * L, db * dn
    hg_heads = state_hgrn.shape[2]
    n_slots = cache_w1_k.shape[3]
    n_pat = len(DIL_PATTERNS)
    M = mem_prompt.shape[1]
    hg_cols = (0, hg_heads, 2 * hg_heads, 3 * hg_heads)
    qcol, kcol, vcol = (4 * hg_heads + s * n_pat * n_slots for s in range(3))
    lbs = jnp.cumsum(jax.nn.softmax(hg_lower_bound.astype(F32), axis=0), axis=0)
    slopes = _alibi_slopes(n_pat, n_slots)
    cache_k = (cache_w1_k, cache_w2_k, cache_w3_k)
    cache_v = (cache_w1_v, cache_w2_v, cache_w3_v)
    prompt_units = (4, 1, 1)

    h = jnp.concatenate([x_prompt.reshape(n_p, D), x_sample.reshape(n_s, D)], axis=0)
    hgp, hgs, mkp, mvp = [], [], [], []
    wkp, wvp, wks, wvs = ([[] for _ in DIL_PATTERNS] for _ in range(4))
    for l in range(depth):
        lb, ng = lbs[l].reshape(1, -1), hg_norm_g[l].reshape(1, -1)
        proj = matmul([rmsnorm(h, norm_mix_g[l])], w_in[l])

        def head_cols(col0, p, rows):
            c0 = (col0 + p * n_slots) * HEAD_DIM
            return rows[:, c0:c0 + n_slots * HEAD_DIM]

        o_hg_p, s_p = hgrn_prompt(proj, lb, ng, b, L, hg_heads, hg_cols)
        outs, lses = [], []
        for p, (win, dil) in enumerate(DIL_PATTERNS):
            o, lse = dilated_attention_prompt(proj, slopes[p], b, L, n_slots, qcol + p * n_slots, kcol + p * n_slots,
                                              vcol + p * n_slots, win, dil, prompt_units[p])
            outs.append(o)
            lses.append(lse)
            keep = min(win, L)
            for col0, dst in ((kcol, wkp), (vcol, wvp)):
                rows = head_cols(col0, p, proj[:n_p]).reshape(b, L, n_slots, HEAD_DIM)
                dst[p].append(rows[:, L - keep:])
        o_at_p = merge_patterns(outs, lses)
        hgp.append(s_p)

        o_hg_s, s_s = hgrn_sample(proj, n_p, state_hgrn[l:l + 1], lb, ng, dn, hg_cols)
        hgs.append(s_s[0])
        proj_s = proj[n_p:]
        outs, lses = [], []
        for p, (win, dil) in enumerate(DIL_PATTERNS):
            q_s, k_s, v_s = (head_cols(c, p, proj_s).reshape(db, dn, n_slots, HEAD_DIM) for c in (qcol, kcol, vcol))
            ck, cv = cache_k[p][l:l + 1], cache_v[p][l:l + 1]
            o, lse = dilated_attention_sample(q_s, k_s, v_s, ck, cv, slopes[p], win, dil)
            outs.append(o.reshape(n_s, n_slots * HEAD_DIM))
            lses.append(lse.reshape(n_s, n_slots * HEAD_DIM))
            wks[p].append(cache_shift(ck, k_s)[0])
            wvs[p].append(cache_shift(cv, v_s)[0])
        o_at_s = merge_patterns(outs, lses)

        mix_hg = jnp.concatenate([o_hg_p, o_hg_s], axis=0)
        mix_at = jnp.concatenate([o_at_p, o_at_s], axis=0)
        h = matmul([mix_hg, mix_at], w_out[l], res=h)

        nm = rmsnorm(mem_prompt.reshape(b * M, D), norm_mem_g[l])
        mk, mv = matmul([nm], wk_x[l]), matmul([nm], wv_x[l])
        mkp.append(mk.reshape(b, M, MEM_HEADS, D // MEM_HEADS))
        mvp.append(mv.reshape(b, M, MEM_HEADS, D // MEM_HEADS))
        qx = matmul([rmsnorm(h, norm_x_g[l])], wq_x[l])
        a_p = mem_attention_prompt(qx, mk, mv, b, L, M, MEM_HEADS)
        a_s = mem_attention_sample(qx[n_p:].reshape(db, dn, MEM_HEADS, D // MEM_HEADS),
                                   cache_mem_k[l:l + 1], cache_mem_v[l:l + 1])
        att = jnp.concatenate([a_p, a_s.reshape(n_s, D).astype(BF16)], axis=0)
        h = matmul([att], wo_x[l], res=h)

        n3 = rmsnorm(h, norm_ffn_g[l])
        gates = peer_router(matmul([n3], peer_wq[l]), peer_subkeys[l])
        ffn = peer_experts(n3, gates, cast(peer_u[l], BF16), cast(peer_v[l], BF16))
        if l + 1 < depth:
            h = add_rmsnorm(h, ffn, None)
    y = add_rmsnorm(h, ffn, norm_final_g)
    y_prompt, y_sample = y[:n_p].reshape(b, L, D), y[n_p:].reshape(db, dn, D)
    st = lambda xs: jnp.stack(xs)
    return (y_prompt, y_sample, st(hgp), st(wkp[0]), st(wvp[0]), st(wkp[1]), st(wvp[1]), st(wkp[2]), st(wvp[2]),
            st(mkp), st(mvp), st(hgs), st(wks[0]), st(wvs[0]), st(wks[1]), st(wvs[1]), st(wks[2]), st(wvs[2]))
```

```python
import functools
import math

import jax
import jax.numpy as jnp
from jax import lax
from jax.experimental import pallas as pl
from jax.experimental.pallas import tpu as pltpu

F32 = jnp.float32
BF16 = jnp.bfloat16
NEG_INF = -1e30
RMS_EPS = 1e-6
HEAD_DIM = 128
HG_CHUNK = 16
DIL_PATTERNS = ((128, 1), (512, 4), (2048, 16))
ALIBI_MAX = 8.0
MEM_HEADS = 4
PK_HEADS = 8
N_KEYS = 128
PK_TOPK = 16

VMEM_LIMIT_BYTES = 56 * 1024 * 1024


def _params(*sem):
    return pltpu.CompilerParams(dimension_semantics=sem, vmem_limit_bytes=VMEM_LIMIT_BYTES)


def _row_tile(n, target):
    best = None
    for t in range(8, min(n, target) + 1, 8):
        if n % t == 0:
            best = t
    assert best is not None, (n, target)
    return best


def _rmsnorm_kernel(x_ref, g_ref, o_ref):
    x = x_ref[...]
    ms = jnp.mean(x * x, axis=-1, keepdims=True)
    o_ref[...] = (x * lax.rsqrt(ms + RMS_EPS) * g_ref[...]).astype(o_ref.dtype)


def rmsnorm(x, g, out_dtype=BF16, tm=512):
    T, D = x.shape
    tm = _row_tile(T, tm)
    return pl.pallas_call(
        _rmsnorm_kernel,
        out_shape=jax.ShapeDtypeStruct((T, D), out_dtype),
        grid=(T // tm,),
        in_specs=[pl.BlockSpec((tm, D), lambda i: (i, 0)), pl.BlockSpec((1, D), lambda i: (0, 0))],
        out_specs=pl.BlockSpec((tm, D), lambda i: (i, 0)),
        compiler_params=_params("parallel"),
        name="rmsnorm",
    )(x, g.reshape(1, D).astype(F32))


def _rmsnorm_cat_kernel(xa_ref, xb_ref, g_ref, n_ref, h_ref, *, blocks_a):
    def emit(x):
        h_ref[...] = x
        ms = jnp.mean(x * x, axis=-1, keepdims=True)
        n_ref[...] = (x * lax.rsqrt(ms + RMS_EPS) * g_ref[...]).astype(n_ref.dtype)

    @pl.when(pl.program_id(0) < blocks_a)
    def _():
        emit(xa_ref[...])

    @pl.when(pl.program_id(0) >= blocks_a)
    def _():
        emit(xb_ref[...])


def rmsnorm_cat(xa, xb, g, tm=256):
    (na, D), nb = xa.shape, xb.shape[0]
    tm = _row_tile(math.gcd(na, nb), tm)
    ba, T = na // tm, na + nb
    out = pl.BlockSpec((tm, D), lambda i: (i, 0))
    return pl.pallas_call(
        functools.partial(_rmsnorm_cat_kernel, blocks_a=ba),
        out_shape=(jax.ShapeDtypeStruct((T, D), BF16), jax.ShapeDtypeStruct((T, D), F32)),
        grid=(T // tm,),
        in_specs=[pl.BlockSpec((tm, D), lambda i: (jnp.minimum(i, ba - 1), 0)),
                  pl.BlockSpec((tm, D), lambda i: (jnp.maximum(i - ba, 0), 0)),
                  pl.BlockSpec((1, D), lambda i: (0, 0))],
        out_specs=(out, out),
        compiler_params=_params("arbitrary"),
        name="rmsnorm_cat",
    )(xa, xb, g.reshape(1, D).astype(F32))


def _add_rmsnorm_split_kernel(x_ref, y_ref, g_ref, oa_ref, ob_ref, *, blocks_a):
    x = x_ref[...] + y_ref[...]
    ms = jnp.mean(x * x, axis=-1, keepdims=True)
    x = x * lax.rsqrt(ms + RMS_EPS) * g_ref[...]

    @pl.when(pl.program_id(0) < blocks_a)
    def _():
        oa_ref[...] = x

    @pl.when(pl.program_id(0) >= blocks_a)
    def _():
        ob_ref[...] = x


def add_rmsnorm_split(x, y, g, na, tm=256):
    T, D = x.shape
    tm = _row_tile(math.gcd(na, T - na), tm)
    ba = na // tm
    blk = pl.BlockSpec((tm, D), lambda i: (i, 0))
    return pl.pallas_call(
        functools.partial(_add_rmsnorm_split_kernel, blocks_a=ba),
        out_shape=(jax.ShapeDtypeStruct((na, D), F32), jax.ShapeDtypeStruct((T - na, D), F32)),
        grid=(T // tm,),
        in_specs=[blk, blk, pl.BlockSpec((1, D), lambda i: (0, 0))],
        out_specs=(pl.BlockSpec((tm, D), lambda i: (jnp.minimum(i, ba - 1), 0)),
                   pl.BlockSpec((tm, D), lambda i: (jnp.maximum(i - ba, 0), 0))),
        compiler_params=_params("arbitrary"),
        name="add_rmsnorm_split",
    )(x, y, g.reshape(1, D).astype(F32))


def _add_rmsnorm_kernel(x_ref, y_ref, g_ref, o_ref, *, normalize):
    x = x_ref[...] + y_ref[...]
    if normalize:
        ms = jnp.mean(x * x, axis=-1, keepdims=True)
        x = x * lax.rsqrt(ms + RMS_EPS) * g_ref[...]
    o_ref[...] = x.astype(o_ref.dtype)


def add_rmsnorm(x, y, g, tm=256):
    T, D = x.shape
    tm = _row_tile(T, tm)
    gain = jnp.ones((1, D), F32) if g is None else g.reshape(1, D).astype(F32)
    return pl.pallas_call(
        functools.partial(_add_rmsnorm_kernel, normalize=g is not None),
        out_shape=jax.ShapeDtypeStruct((T, D), F32),
        grid=(T // tm,),
        in_specs=[pl.BlockSpec((tm, D), lambda i: (i, 0)), pl.BlockSpec((tm, D), lambda i: (i, 0)),
                  pl.BlockSpec((1, D), lambda i: (0, 0))],
        out_specs=pl.BlockSpec((tm, D), lambda i: (i, 0)),
        compiler_params=_params("parallel"),
        name="add_rmsnorm",
    )(x, y, gain)


def _mm_kernel(*refs, k_splits, has_res):
    n_a = len(k_splits)
    a_refs, w_ref = refs[:n_a], refs[n_a]
    res_ref = refs[n_a + 1] if has_res else None
    o_ref = refs[-1]
    acc, off = None, 0
    for a_ref, kk in zip(a_refs, k_splits):
        part = jnp.dot(a_ref[...], w_ref[off:off + kk, :].astype(BF16), preferred_element_type=F32)
        acc = part if acc is None else acc + part
        off += kk
    if has_res:
        acc = acc + res_ref[...]
    o_ref[...] = acc.astype(o_ref.dtype)


def matmul(a_list, w, res=None, out_dtype=F32, tm=1088, tn=512):
    M = a_list[0].shape[0]
    k_splits = tuple(a.shape[1] for a in a_list)
    K, N = w.shape
    assert sum(k_splits) == K
    tm, tn = _row_tile(M, tm), min(tn, N)
    assert N % tn == 0
    in_specs = [pl.BlockSpec((tm, kk), lambda i, j: (i, 0)) for kk in k_splits]
    in_specs.append(pl.BlockSpec((K, tn), lambda i, j: (0, j)))
    args = list(a_list) + [w]
    if res is not None:
        in_specs.append(pl.BlockSpec((tm, tn), lambda i, j: (i, j)))
        args.append(res)
    return pl.pallas_call(
        functools.partial(_mm_kernel, k_splits=k_splits, has_res=res is not None),
        out_shape=jax.ShapeDtypeStruct((M, N), out_dtype),
        grid=(M // tm, N // tn),
        in_specs=in_specs,
        out_specs=pl.BlockSpec((tm, tn), lambda i, j: (i, j)),
        compiler_params=_params("parallel", "arbitrary"),
        name="matmul",
    )(*args)


def _silu(x):
    return x * jax.nn.sigmoid(x)


def _split3(x):
    hi = x.astype(BF16)
    r1 = x - hi.astype(F32)
    mid = r1.astype(BF16)
    lo = (r1 - mid.astype(F32)).astype(BF16)
    return hi, mid, lo


def _rows_bcast(x, row_ids, reps):
    W = x.shape[1]
    parts = []
    for r in row_ids:
        row = jnp.zeros((1, W), x.dtype) if r < 0 else x[r:r + 1, :]
        parts.append(jnp.broadcast_to(row, (reps, W)))
    return parts[0] if len(parts) == 1 else jnp.concatenate(parts, axis=0)


def _dot_nt(a, b):
    return lax.dot_general(a, b, (((1,), (1,)), ((), ())), preferred_element_type=F32)


def _dot_tn(a, b):
    return lax.dot_general(a, b, (((0,), (0,)), ((), ())), preferred_element_type=F32)


def _hgrn_chunk(q, k, g, v, st, leaf):
    C = q.shape[0]
    row = lax.broadcasted_iota(jnp.int32, (C, q.shape[1]), 0)
    it = lax.broadcasted_iota(jnp.int32, (C, C), 0)
    js = lax.broadcasted_iota(jnp.int32, (C, C), 1)
    tri = jnp.where(it >= js, 1.0, 0.0).astype(BF16)
    g_hi, g_mid, g_lo = _split3(g)
    cum = (jnp.dot(tri, g_hi, preferred_element_type=F32) + jnp.dot(tri, g_mid, preferred_element_type=F32)
           + jnp.dot(tri, g_lo, preferred_element_type=F32))
    last = cum[C - 1:C, :]

    a = jnp.zeros((C, C), F32)
    bs = C
    while bs > leaf:
        half = bs // 2
        upper = (row % bs) >= half
        cmid = _rows_bcast(cum, [b0 + half - 1 for b0 in range(0, C, bs)], bs)
        e = jnp.exp(jnp.where(upper, cum - cmid, cmid - cum))
        q_l = jnp.where(upper, q * e, 0.0).astype(BF16)
        k_l = jnp.where(upper, 0.0, k * e).astype(BF16)
        a = a + jnp.where((it // bs) == (js // bs), _dot_nt(q_l, k_l), 0.0)
        bs = half
    cstart = _rows_bcast(cum, [b0 - 1 for b0 in range(0, C, leaf)], leaf)
    q_f = (q * jnp.exp(cum - cstart)).astype(BF16)
    k_f = (k * jnp.exp(cstart - cum)).astype(BF16)
    a = a + jnp.where(((it // leaf) == (js // leaf)) & (it >= js), _dot_nt(q_f, k_f), 0.0)

    vb = v.astype(BF16)
    q_g = (q * jnp.exp(cum)).astype(BF16)
    o = jnp.dot(a.astype(BF16), vb, preferred_element_type=F32) + _dot_nt(q_g, st.astype(BF16))
    k_e = (k * jnp.exp(last - cum)).astype(BF16)
    st_new = st * jnp.exp(last) + _dot_tn(vb, k_e)
    return o, st_new


def _hgrn_gates(hq, hf, lb):
    q = _silu(hq)
    f = lb + (1.0 - lb) * jax.nn.sigmoid(hf)
    return q, 1.0 - f, jnp.log(f)


def _hgrn_out(o, hgate, ng):
    ms = jnp.mean(o * o, axis=-1, keepdims=True)
    return o * lax.rsqrt(ms + RMS_EPS) * ng * _silu(hgate)


def _hgrn_prompt_kernel(hq_ref, hf_ref, hi_ref, hg_ref, lb_ref, ng_ref, o_ref, s_ref, st_scr, *, chunk, leaf):
    c = pl.program_id(2)

    @pl.when(c == 0)
    def _():
        st_scr[...] = jnp.zeros_like(st_scr)

    rows = hq_ref.shape[0]
    lb, ng = lb_ref[...], ng_ref[...]
    for s0 in range(0, rows, chunk):
        sl = slice(s0, s0 + chunk)
        q, k, g = _hgrn_gates(hq_ref[sl, :], hf_ref[sl, :], lb)
        o, st_new = _hgrn_chunk(q, k, g, hi_ref[sl, :], st_scr[...], leaf)
        st_scr[...] = st_new
        o_ref[sl, :] = _hgrn_out(o, hg_ref[sl, :], ng).astype(o_ref.dtype)

    @pl.when(c == pl.num_programs(2) - 1)
    def _():
        s_ref[0, 0] = st_scr[...].T


def hgrn_prompt(proj, lb, ng, b, L, n_heads, col_blocks, rows_per_step=512, chunk=128, leaf=HG_CHUNK):
    rows = min(rows_per_step, L)
    assert L % rows == 0 and rows % chunk == 0
    nc = L // rows

    def in_spec(cb):
        return pl.BlockSpec((rows, HEAD_DIM), lambda bi, h, c: (bi * nc + c, cb + h))

    return pl.pallas_call(
        functools.partial(_hgrn_prompt_kernel, chunk=chunk, leaf=leaf),
        out_shape=(jax.ShapeDtypeStruct((b * L, n_heads * HEAD_DIM), BF16),
                   jax.ShapeDtypeStruct((b, n_heads, HEAD_DIM, HEAD_DIM), F32)),
        grid=(b, n_heads, nc),
        in_specs=[in_spec(cb) for cb in col_blocks] + [
            pl.BlockSpec((1, HEAD_DIM), lambda bi, h, c: (0, h)),
            pl.BlockSpec((1, HEAD_DIM), lambda bi, h, c: (0, 0))],
        out_specs=(pl.BlockSpec((rows, HEAD_DIM), lambda bi, h, c: (bi * nc + c, h)),
                   pl.BlockSpec((1, 1, HEAD_DIM, HEAD_DIM), lambda bi, h, c: (bi, h, 0, 0))),
        scratch_shapes=[pltpu.VMEM((HEAD_DIM, HEAD_DIM), F32)],
        compiler_params=_params("parallel", "parallel", "arbitrary"),
        name="hgrn_prompt",
    )(proj, proj, proj, proj, lb, ng)


def _alibi_slopes(n_pat, n_slots):
    k = jnp.arange(1, n_pat * n_slots + 1, dtype=F32)
    return (2.0 ** (-ALIBI_MAX * k / (n_pat * n_slots))).reshape(n_pat, n_slots)


def _strided_rows(ref, start, size, stride):
    if stride == 1:
        return ref[pl.ds(start, size), :]
    return ref[pl.ds(start, size, stride=stride), :]


def _band_softmax(q, kk, vv, bias, valid, scale):
    s = _dot_nt(q.astype(BF16), kk.astype(BF16)) * scale - bias
    s = jnp.where(valid, s, NEG_INF)
    m = jnp.max(s, axis=-1, keepdims=True)
    p = jnp.exp(s - m)
    l = jnp.sum(p, axis=-1, keepdims=True)
    o = jnp.dot(p.astype(BF16), vv.astype(BF16), preferred_element_type=F32) / l
    return o, m + jnp.log(l)


def _dil_attn_kernel(q_ref, kp_ref, kc_ref, vp_ref, vc_ref, slope_ref, o_ref, lse_ref, *, dil, band, units):
    first = pl.program_id(2) == 0
    span = band * dil
    scale = HEAD_DIM ** -0.5
    qi = lax.broadcasted_iota(jnp.int32, (band, 2 * band), 0)
    kj = lax.broadcasted_iota(jnp.int32, (band, 2 * band), 1)
    off = qi + band - kj
    in_band = (off >= 0) & (off <= band)
    bias = slope_ref[0] * float(dil) * off.astype(F32)
    for u in range(units):
        for r in range(dil):
            base = u * span + r
            q = _strided_rows(q_ref, base, band, dil)
            if u == 0:
                kp, vp = _strided_rows(kp_ref, r, band, dil), _strided_rows(vp_ref, r, band, dil)
                valid = in_band & (jnp.logical_not(first) | (kj >= band))
            else:
                kp, vp = _strided_rows(kc_ref, base - span, band, dil), _strided_rows(vc_ref, base - span, band, dil)
                valid = in_band
            kk = jnp.concatenate([kp, _strided_rows(kc_ref, base, band, dil)], axis=0)
            vv = jnp.concatenate([vp, _strided_rows(vc_ref, base, band, dil)], axis=0)
            o, lse = _band_softmax(q, kk, vv, bias, valid, scale)
            lse_b = jnp.broadcast_to(lse, (band, HEAD_DIM))
            if dil == 1:
                o_ref[pl.ds(base, band), :] = o
                lse_ref[pl.ds(base, band), :] = lse_b
            else:
                o_ref[pl.ds(base, band, stride=dil), :] = o
                lse_ref[pl.ds(base, band, stride=dil), :] = lse_b


def dilated_attention_prompt(proj, slopes_p, b, L, n_slots, qcol, kcol, vcol, win, dil, units):
    band = win // dil
    span = band * dil
    rows = span * units
    assert L % rows == 0
    nblk = L // rows
    cur = lambda c0: pl.BlockSpec((rows, HEAD_DIM), lambda bi, h, i: (bi * nblk + i, c0 + h))
    prev = lambda c0: pl.BlockSpec(
        (span, HEAD_DIM), lambda bi, h, i: (bi * nblk * units + jnp.maximum(i * units - 1, 0), c0 + h))
    out = pl.BlockSpec((rows, HEAD_DIM), lambda bi, h, i: (bi * nblk + i, h))
    shape = jax.ShapeDtypeStruct((b * L, n_slots * HEAD_DIM), F32)
    slope_rep = jnp.broadcast_to(slopes_p.reshape(n_slots, 1, 1), (n_slots, 1, 2 * band)).astype(F32)
    return pl.pallas_call(
        functools.partial(_dil_attn_kernel, dil=dil, band=band, units=units),
        out_shape=(shape, shape),
        grid=(b, n_slots, nblk),
        in_specs=[cur(qcol), prev(kcol), cur(kcol), prev(vcol), cur(vcol),
                  pl.BlockSpec((1, 1, 2 * band), lambda bi, h, i: (h, 0, 0))],
        out_specs=(out, out),
        compiler_params=_params("parallel", "parallel", "arbitrary"),
        name=f"dil_attn_prompt_d{dil}",
    )(proj, proj, proj, proj, proj, slope_rep)


def _merge_kernel(*refs):
    n = (len(refs) - 1) // 2
    o_refs, l_refs, out_ref = refs[:n], refs[n:2 * n], refs[-1]
    ls = [r[...] for r in l_refs]
    m = functools.reduce(jnp.maximum, ls)
    ws = [jnp.exp(l - m) for l in ls]
    num = functools.reduce(lambda a, c: a + c, [w * r[...] for w, r in zip(ws, o_refs)])
    den = functools.reduce(lambda a, c: a + c, ws)
    out_ref[...] = (num / den).astype(out_ref.dtype)


def merge_patterns(outs, lses, tm=512):
    T, W = outs[0].shape
    tm = _row_tile(T, tm)
    spec = pl.BlockSpec((tm, W), lambda i: (i, 0))
    return pl.pallas_call(
        _merge_kernel,
        out_shape=jax.ShapeDtypeStruct((T, W), BF16),
        grid=(T // tm,),
        in_specs=[spec] * (2 * len(outs)),
        out_specs=spec,
        compiler_params=_params("parallel"),
        name="merge_patterns",
    )(*outs, *lses)


def _mem_attn_kernel(q_ref, k_ref, v_ref, o_ref):
    scale = q_ref.shape[-1] ** -0.5
    s = _dot_nt(q_ref[...].astype(BF16), k_ref[...].astype(BF16)) * scale
    m = jnp.max(s, axis=-1, keepdims=True)
    p = jnp.exp(s - m)
    l = jnp.sum(p, axis=-1, keepdims=True)
    o = jnp.dot(p.astype(BF16), v_ref[...].astype(BF16), preferred_element_type=F32) / l
    o_ref[...] = o.astype(o_ref.dtype)


def mem_attention_prompt(q, mk, mv, b, L, M, n_heads, tq=1024):
    D = mk.shape[1]
    hd = D // n_heads
    tq = _row_tile(L, tq)
    nq = L // tq
    return pl.pallas_call(
        _mem_attn_kernel,
        out_shape=jax.ShapeDtypeStruct((b * L, D), BF16),
        grid=(b, n_heads, nq),
        in_specs=[pl.BlockSpec((tq, hd), lambda bi, h, i: (bi * nq + i, h)),
                  pl.BlockSpec((M, hd), lambda bi, h, i: (bi, h)),
                  pl.BlockSpec((M, hd), lambda bi, h, i: (bi, h))],
        out_specs=pl.BlockSpec((tq, hd), lambda bi, h, i: (bi * nq + i, h)),
        compiler_params=_params("parallel", "parallel", "arbitrary"),
        name="mem_attn_prompt",
    )(q, mk, mv)


def _dot_nt_x3(a, b):
    a_hi, b_hi = a.astype(BF16), b.astype(BF16)
    a_lo, b_lo = (a - a_hi.astype(F32)).astype(BF16), (b - b_hi.astype(F32)).astype(BF16)
    return _dot_nt(a_hi, b_hi) + _dot_nt(a_hi, b_lo) + _dot_nt(a_lo, b_hi)


def _topk_rows(x, k, payload=None):
    n = x.shape[0]
    iota = lax.broadcasted_iota(jnp.int32, x.shape, 0).astype(F32)
    vals, outs = [], []
    for _ in range(k):
        m = jnp.max(x, axis=0, keepdims=True)
        pos = jnp.min(jnp.where(x == m, iota, float(n)), axis=0, keepdims=True)
        sel = iota == pos
        vals.append(m)
        outs.append(pos if payload is None else jnp.max(jnp.where(sel, payload, -1.0), axis=0, keepdims=True))
        x = jnp.where(sel, -jnp.inf, x)
    return jnp.concatenate(vals, axis=0), jnp.concatenate(outs, axis=0)


def _candidates(v1, i1, v2, i2, kk):
    vals, idxs, a = [], [], 0
    while a < kk and kk // (a + 1) >= 2:
        nb = min(kk, -(-(kk // (a + 1)) // 8) * 8)
        vals.append(v1[a:a + 1, :] + v2[:nb, :])
        idxs.append(i1[a:a + 1, :] * float(N_KEYS) + i2[:nb, :])
        a += 1
    if a < kk:
        vals.append(v1[a:, :] + v2[0:1, :])
        idxs.append(i1[a:, :] * float(N_KEYS) + i2[0:1, :])
    return jnp.concatenate(vals, axis=0), jnp.concatenate(idxs, axis=0)


_STAGE_PITCH = N_KEYS + 8


def _router_kernel(qp_ref, sk_ref, g_ref, ii_scr, ij_scr, w_scr, iit_scr, ijt_scr, wt_scr, stage_scr):
    tb = qp_ref.shape[0]
    kk = PK_TOPK
    for h in range(PK_HEADS):
        tops = []
        for c in range(2):
            col = (2 * h + c) * HEAD_DIM
            sc = _dot_nt_x3(sk_ref[2 * h + c], qp_ref[:, col:col + HEAD_DIM])
            tops.append(_topk_rows(sc, kk))
        (v1, i1), (v2, i2) = tops
        cand, cidx = _candidates(v1, i1, v2, i2, kk)
        best, eidx = _topk_rows(cand, kk, payload=cidx)
        e = jnp.exp(best - best[0:1, :])
        gate = e / jnp.sum(e, axis=0, keepdims=True)
        ei = jnp.floor(eidx * (1.0 / N_KEYS))
        ii_scr[h * kk:(h + 1) * kk, :] = ei
        ij_scr[h * kk:(h + 1) * kk, :] = eidx - ei * float(N_KEYS)
        w_scr[h * kk:(h + 1) * kk, :] = gate
    iit_scr[...] = ii_scr[...].T
    ijt_scr[...] = ij_scr[...].T
    wt_scr[...] = w_scr[...].T
    sub = lax.broadcasted_iota(jnp.int32, (N_KEYS, PK_HEADS * kk), 0).astype(F32)

    def per_token(t, carry):
        a = jnp.where(sub == iit_scr[pl.ds(t, 1), :], wt_scr[pl.ds(t, 1), :], 0.0).astype(BF16)
        bsel = jnp.where(sub == ijt_scr[pl.ds(t, 1), :], 1.0, 0.0).astype(BF16)
        row0 = pl.multiple_of(t * _STAGE_PITCH, 8)
        stage_scr[pl.ds(row0, N_KEYS), :] = _dot_nt(a, bsel)
        return carry

    lax.fori_loop(0, tb, per_token, 0, unroll=8)
    for i in range(N_KEYS):
        g_ref[:, i * N_KEYS:(i + 1) * N_KEYS] = stage_scr[pl.ds(i, tb, stride=_STAGE_PITCH), :].astype(g_ref.dtype)


def peer_router(qp, subkeys, tb=128):
    T = qp.shape[0]
    assert T % tb == 0 and tb == N_KEYS
    sk = subkeys.reshape(PK_HEADS * 2, N_KEYS, HEAD_DIM)
    slots = PK_HEADS * PK_TOPK
    return pl.pallas_call(
        _router_kernel,
        out_shape=jax.ShapeDtypeStruct((T, N_KEYS * N_KEYS), BF16),
        grid=(T // tb,),
        in_specs=[pl.BlockSpec((tb, qp.shape[1]), lambda i: (i, 0)),
                  pl.BlockSpec(sk.shape, lambda i: (0, 0, 0))],
        out_specs=pl.BlockSpec((tb, N_KEYS * N_KEYS), lambda i: (i, 0)),
        scratch_shapes=[pltpu.VMEM((slots, tb), F32)] * 3 + [pltpu.VMEM((tb, slots), F32)] * 3
        + [pltpu.VMEM((tb * _STAGE_PITCH, N_KEYS), F32)],
        compiler_params=_params("parallel"),
        name="peer_router",
    )(qp, sk)


def _gelu(x):
    return 0.5 * x * (1.0 + lax.erf(x * (2.0 ** -0.5)))


def _cast_kernel(x_ref, o_ref):
    o_ref[...] = x_ref[...].astype(o_ref.dtype)


def cast(x, dtype, tm=512):
    R, C = x.shape
    tm = _row_tile(R, tm)
    spec = pl.BlockSpec((tm, C), lambda i: (i, 0))
    return pl.pallas_call(
        _cast_kernel, out_shape=jax.ShapeDtypeStruct((R, C), dtype), grid=(R // tm,), in_specs=[spec],
        out_specs=spec, compiler_params=_params("parallel"), name="cast",
    )(x)


def _peer_kernel(n_ref, g_ref, u_ref, v_ref, o_ref, hid_a, hid_b):
    j = pl.program_id(1)

    @pl.when(j == 0)
    def _():
        hid_b[...] = jnp.zeros_like(hid_b)
        o_ref[...] = jnp.zeros_like(o_ref)

    def step(prev_scr, next_scr):
        part = jnp.dot(prev_scr[...], v_ref[...], preferred_element_type=F32)
        xu = _dot_nt(n_ref[...], u_ref[...])
        next_scr[...] = (g_ref[...].astype(F32) * _gelu(xu)).astype(BF16)
        o_ref[...] += part

    @pl.when(j % 2 == 0)
    def _():
        step(hid_b, hid_a)

    @pl.when(j % 2 == 1)
    def _():
        step(hid_a, hid_b)


def peer_experts(n, gates, u, v, tm=544, te=256):
    T, D = n.shape
    E = u.shape[0]
    tm = _row_tile(T, tm)
    assert E % te == 0
    ne = E // te
    return pl.pallas_call(
        _peer_kernel,
        out_shape=jax.ShapeDtypeStruct((T, D), F32),
        grid=(T // tm, ne + 1),
        in_specs=[pl.BlockSpec((tm, D), lambda i, j: (i, 0)),
                  pl.BlockSpec((tm, te), lambda i, j: (i, jnp.minimum(j, ne - 1))),
                  pl.BlockSpec((te, D), lambda i, j: (jnp.minimum(j, ne - 1), 0)),
                  pl.BlockSpec((te, D), lambda i, j: (jnp.maximum(j - 1, 0), 0))],
        out_specs=pl.BlockSpec((tm, D), lambda i, j: (i, 0)),
        scratch_shapes=[pltpu.VMEM((tm, te), BF16)] * 2,
        compiler_params=_params("parallel", "arbitrary"),
        name="peer_experts",
    )(n, gates, u, v)


def _hgrn_sample_kernel(hq_ref, hf_ref, hi_ref, hg_ref, lb_ref, ng_ref, s_ref, o_ref, so_ref, *, n_tok):
    rows, width = hq_ref.shape
    n_seq, n_head = rows // n_tok, width // HEAD_DIM
    q, k, g = _hgrn_gates(hq_ref[...], hf_ref[...], lb_ref[...])
    t = lax.broadcasted_iota(jnp.int32, (rows, width), 0) % n_tok
    cum, sh = g, 1
    while sh < n_tok:
        cum = cum + jnp.where(t >= sh, pltpu.roll(cum, sh, axis=0), 0.0)
        sh *= 2
    last = _rows_bcast(cum, [s * n_tok + n_tok - 1 for s in range(n_seq)], n_tok)
    q_in = q * jnp.exp(cum)
    k_in = k * jnp.exp(-cum)
    k_out = k * jnp.exp(last - cum)
    dec_t = jnp.exp(last).T
    v = hi_ref[...]
    it = lax.broadcasted_iota(jnp.int32, (n_tok, n_tok), 0)
    js = lax.broadcasted_iota(jnp.int32, (n_tok, n_tok), 1)
    ng = ng_ref[...]
    for s in range(n_seq):
        r = slice(s * n_tok, (s + 1) * n_tok)
        for h in range(n_head):
            c = slice(h * HEAD_DIM, (h + 1) * HEAD_DIM)
            st = s_ref[0, s, h]
            qh, vh = q_in[r, c].astype(BF16), v[r, c].astype(BF16)
            a = jnp.where(it >= js, _dot_nt(qh, k_in[r, c].astype(BF16)), 0.0)
            o = jnp.dot(a.astype(BF16), vh, preferred_element_type=F32)
            o = o + jnp.dot(qh, st.astype(BF16), preferred_element_type=F32)
            dec = dec_t[c, s * n_tok:s * n_tok + 1]
            so_ref[0, s, h] = st * dec + _dot_tn(k_out[r, c].astype(BF16), vh)
            o_ref[r, c] = _hgrn_out(o, hg_ref[r, c], ng).astype(o_ref.dtype)


def hgrn_sample(proj, row0, state, lb, ng, n_tok, col_blocks, seqs_per_step=4, heads_per_step=4):
    _, n_seq, n_heads, _, _ = state.shape
    rows, width = seqs_per_step * n_tok, heads_per_step * HEAD_DIM
    assert n_seq % seqs_per_step == 0 and n_heads % heads_per_step == 0 and row0 % rows == 0
    r0, nhb = row0 // rows, n_heads // heads_per_step

    def in_spec(cb):
        return pl.BlockSpec((rows, width), lambda i, j: (r0 + i, cb // heads_per_step + j))

    st_spec = pl.BlockSpec((1, seqs_per_step, heads_per_step, HEAD_DIM, HEAD_DIM), lambda i, j: (0, i, j, 0, 0))
    return pl.pallas_call(
        functools.partial(_hgrn_sample_kernel, n_tok=n_tok),
        out_shape=(jax.ShapeDtypeStruct((n_seq * n_tok, n_heads * HEAD_DIM), BF16),
                   jax.ShapeDtypeStruct(state.shape, F32)),
        grid=(n_seq // seqs_per_step, nhb),
        in_specs=[in_spec(cb) for cb in col_blocks] + [
            pl.BlockSpec((1, width), lambda i, j: (0, j)),
            pl.BlockSpec((1, HEAD_DIM), lambda i, j: (0, 0)),
            st_spec],
        out_specs=(pl.BlockSpec((rows, width), lambda i, j: (i, j)), st_spec),
        compiler_params=_params("parallel", "parallel"),
        name="hgrn_sample",
    )(proj, proj, proj, proj, lb, ng, state)


def _lane_sum_rep(x):
    ones = jnp.ones((x.shape[1], HEAD_DIM), BF16)
    hi = x.astype(BF16)
    lo = (x - hi.astype(F32)).astype(BF16)
    return jnp.dot(hi, ones, preferred_element_type=F32) + jnp.dot(lo, ones, preferred_element_type=F32)


def _dil_attn_sample_kernel(q_ref, kn_ref, vn_ref, ck_ref, cv_ref, ckx_ref, cvx_ref, slope_ref,
                            o_ref, lse_ref, ok_ref, ov_ref, m_scr, l_scr, acc_scr, *, dil, band):
    n, H, D = q_ref.shape[1:]
    rc = ck_ref.shape[2]
    c, nc = pl.program_id(1), pl.num_programs(1)
    scale = D ** -0.5
    slope = slope_ref[...] * float(dil)

    last = c == nc - 1
    for src, nxt, new, dst in ((ck_ref, ckx_ref, kn_ref, ok_ref), (cv_ref, cvx_ref, vn_ref, ov_ref)):
        dst[0, 0, 0:rc - n] = src[0, 0, n:rc]
        dst[0, 0, rc - n:rc] = jnp.where(last, new[0], nxt[0, 0])

    @pl.when(c == 0)
    def _():
        for i in range(n):
            q = q_ref[0, i]
            news = [ip for ip in range(i + 1) if (i - ip) % dil == 0]
            s_n = [_lane_sum_rep(kn_ref[0, ip] * q) * scale - slope * float((i - ip) // dil) for ip in news]
            m = functools.reduce(jnp.maximum, s_n)
            p_n = [jnp.exp(s - m) for s in s_n]
            m_scr[i] = m
            l_scr[i] = functools.reduce(lambda a, b: a + b, p_n)
            acc_scr[i] = functools.reduce(lambda a, b: a + b, [p * vn_ref[0, ip] for p, ip in zip(p_n, news)])

    rows = rc // dil if dil >= n else band
    mm = lax.broadcasted_iota(jnp.int32, (rows, H, D), 0)
    for i in range(n):
        q = q_ref[0, i]
        if dil >= n:
            kc = ck_ref[0, 0, pl.ds(i, rows, stride=dil)]
            vc = cv_ref[0, 0, pl.ds(i, rows, stride=dil)]
            jc = (nc - c) * rows - mm
            ok_c = None
        else:
            kc = ck_ref[0, 0, pl.ds(rc - band, band)]
            vc = cv_ref[0, 0, pl.ds(rc - band, band)]
            jc = band + i - mm
            ok_c = jc <= band
        s_c = _lane_sum_rep((kc * q[None]).reshape(rows * H, D)).reshape(rows, H, D) * scale
        s_c = s_c - slope[None] * jc.astype(F32)
        if ok_c is not None:
            s_c = jnp.where(ok_c, s_c, NEG_INF)
        m_old = m_scr[i]
        m_new = jnp.maximum(m_old, jnp.max(s_c, axis=0))
        alpha = jnp.exp(m_old - m_new)
        p_c = jnp.exp(s_c - m_new[None])
        l_scr[i] = l_scr[i] * alpha + jnp.sum(p_c, axis=0)
        acc_scr[i] = acc_scr[i] * alpha + jnp.sum(p_c * vc, axis=0)
        m_scr[i] = m_new

    @pl.when(last)
    def _():
        for i in range(n):
            o_ref[0, i] = acc_scr[i] / l_scr[i]
            lse_ref[0, i] = m_scr[i] + jnp.log(l_scr[i])


def dilated_attention_sample(q, k_new, v_new, cache_k, cache_v, slopes_p, win, dil, chunk_rows=512):
    B, n, H, D = q.shape
    keep = cache_k.shape[2]
    band = win // dil
    assert keep == band * dil and keep > n and keep % n == 0, "window cache must hold exactly one window"
    rc = keep if dil < n else min(keep, chunk_rows)
    assert keep % rc == 0 and rc % dil == 0 and rc % n == 0 and (dil >= n or dil == 1)
    nc = keep // rc
    new_spec = pl.BlockSpec((1, n, H, D), lambda b, c: (b, 0, 0, 0))
    c_spec = pl.BlockSpec((1, 1, rc, H, D), lambda b, c: (0, b, c, 0, 0))
    x_spec = pl.BlockSpec((1, 1, n, H, D), lambda b, c: (0, b, jnp.minimum((c + 1) * (rc // n), keep // n - 1), 0, 0))
    shape = jax.ShapeDtypeStruct((B, n, H, D), F32)
    slope_rep = jnp.broadcast_to(slopes_p.reshape(H, 1), (H, D)).astype(F32)
    return pl.pallas_call(
        functools.partial(_dil_attn_sample_kernel, dil=dil, band=band),
        out_shape=(shape, shape, jax.ShapeDtypeStruct(cache_k.shape, cache_k.dtype),
                   jax.ShapeDtypeStruct(cache_v.shape, cache_v.dtype)),
        grid=(B, nc),
        in_specs=[new_spec, new_spec, new_spec, c_spec, c_spec, x_spec, x_spec,
                  pl.BlockSpec((H, D), lambda b, c: (0, 0))],
        out_specs=(new_spec, new_spec, c_spec, c_spec),
        scratch_shapes=[pltpu.VMEM((n, H, D), F32)] * 3,
        compiler_params=_params("parallel", "arbitrary"),
        name=f"dil_attn_sample_d{dil}",
    )(q, k_new, v_new, cache_k, cache_v, cache_k, cache_v, slope_rep)


def _mem_attn_sample_kernel(q_ref, k_ref, v_ref, o_ref):
    n, H, D = q_ref.shape[1:]
    M = k_ref.shape[2]
    scale = D ** -0.5
    k2 = k_ref[0, 0].reshape(M * H, D).astype(BF16)
    v2 = v_ref[0, 0].reshape(M * H, D).astype(BF16)
    q2 = q_ref[0].reshape(n * H, D).astype(BF16)
    s = _dot_nt(q2, k2) * scale
    qh = lax.broadcasted_iota(jnp.int32, s.shape, 0) % H
    kh = lax.broadcasted_iota(jnp.int32, s.shape, 1) % H
    s = jnp.where(qh == kh, s, NEG_INF)
    m = jnp.max(s, axis=-1, keepdims=True)
    p = jnp.exp(s - m)
    l = jnp.sum(p, axis=-1, keepdims=True)
    o = jnp.dot(p.astype(BF16), v2, preferred_element_type=F32) / l
    o_ref[0] = o.reshape(n, H, D)


def mem_attention_sample(q, mem_k, mem_v):
    B, n, H, D = q.shape
    M = mem_k.shape[2]
    q_spec = pl.BlockSpec((1, n, H, D), lambda i: (i, 0, 0, 0))
    m_spec = pl.BlockSpec((1, 1, M, H, D), lambda i: (0, i, 0, 0, 0))
    return pl.pallas_call(
        _mem_attn_sample_kernel,
        out_shape=jax.ShapeDtypeStruct((B, n, H, D), F32),
        grid=(B,),
        in_specs=[q_spec, m_spec, m_spec],
        out_specs=q_spec,
        compiler_params=_params("parallel"),
        name="mem_attn_sample",
    )(q, mem_k, mem_v)


def kernel(x_prompt, x_sample, mem_prompt, state_hgrn, cache_w1_k, cache_w1_v, cache_w2_k, cache_w2_v, cache_w3_k, cache_w3_v, cache_mem_k, cache_mem_v, norm_mix_g, w_in, hg_lower_bound, hg_norm_g, w_out, norm_x_g, norm_mem_g, wq_x, wk_x, wv_x, wo_x, norm_ffn_g, peer_wq, peer_subkeys, peer_u, peer_v, norm_final_g):
    b, L, D = x_prompt.shape
    db, dn, _ = x_sample.shape
    depth = w_in.shape[0]
    n_p, n_s = b * L, db * dn
    hg_heads = state_hgrn.shape[2]
    n_slots = cache_w1_k.shape[3]
    n_pat = len(DIL_PATTERNS)
    M = mem_prompt.shape[1]
    hg_cols = (0, hg_heads, 2 * hg_heads, 3 * hg_heads)
    qcol, kcol, vcol = (4 * hg_heads + s * n_pat * n_slots for s in range(3))
    lbs = jnp.cumsum(jax.nn.softmax(hg_lower_bound.astype(F32), axis=0), axis=0)
    slopes = _alibi_slopes(n_pat, n_slots)
    cache_k = (cache_w1_k, cache_w2_k, cache_w3_k)
    cache_v = (cache_w1_v, cache_w2_v, cache_w3_v)
    prompt_units = (4, 1, 1)

    n1, h = rmsnorm_cat(x_prompt.reshape(n_p, D), x_sample.reshape(n_s, D), norm_mix_g[0])
    hgp, hgs, mkp, mvp = [], [], [], []
    wkp, wvp, wks, wvs = ([[] for _ in DIL_PATTERNS] for _ in range(4))
    for l in range(depth):
        lb, ng = lbs[l].reshape(1, -1), hg_norm_g[l].reshape(1, -1)
        proj = matmul([n1 if l == 0 else rmsnorm(h, norm_mix_g[l])], w_in[l])

        def head_cols(col0, p, rows):
            c0 = (col0 + p * n_slots) * HEAD_DIM
            return rows[:, c0:c0 + n_slots * HEAD_DIM]

        o_hg_p, s_p = hgrn_prompt(proj, lb, ng, b, L, hg_heads, hg_cols)
        outs, lses = [], []
        for p, (win, dil) in enumerate(DIL_PATTERNS):
            o, lse = dilated_attention_prompt(proj, slopes[p], b, L, n_slots, qcol + p * n_slots, kcol + p * n_slots,
                                              vcol + p * n_slots, win, dil, prompt_units[p])
            outs.append(o)
            lses.append(lse)
            keep = min(win, L)
            for col0, dst in ((kcol, wkp), (vcol, wvp)):
                tails = [head_cols(col0, p, proj[(bi + 1) * L - keep:(bi + 1) * L]) for bi in range(b)]
                dst[p].append(jnp.stack(tails).reshape(b, keep, n_slots, HEAD_DIM))
        o_at_p = merge_patterns(outs, lses)
        hgp.append(s_p)

        o_hg_s, s_s = hgrn_sample(proj, n_p, state_hgrn[l:l + 1], lb, ng, dn, hg_cols)
        hgs.append(s_s[0])
        proj_s = proj[n_p:]
        outs, lses = [], []
        for p, (win, dil) in enumerate(DIL_PATTERNS):
            q_s, k_s, v_s = (head_cols(c, p, proj_s).reshape(db, dn, n_slots, HEAD_DIM) for c in (qcol, kcol, vcol))
            ck, cv = cache_k[p][l:l + 1], cache_v[p][l:l + 1]
            o, lse, nk, nv = dilated_attention_sample(q_s, k_s, v_s, ck, cv, slopes[p], win, dil)
            outs.append(o.reshape(n_s, n_slots * HEAD_DIM))
            lses.append(lse.reshape(n_s, n_slots * HEAD_DIM))
            wks[p].append(nk[0])
            wvs[p].append(nv[0])
        o_at_s = merge_patterns(outs, lses)

        mix_hg = jnp.concatenate([o_hg_p, o_hg_s], axis=0)
        mix_at = jnp.concatenate([o_at_p, o_at_s], axis=0)
        h = matmul([mix_hg, mix_at], w_out[l], res=h)

        nm = rmsnorm(mem_prompt.reshape(b * M, D), norm_mem_g[l])
        mk, mv = matmul([nm], wk_x[l]), matmul([nm], wv_x[l])
        mkp.append(mk.reshape(b, M, MEM_HEADS, D // MEM_HEADS))
        mvp.append(mv.reshape(b, M, MEM_HEADS, D // MEM_HEADS))
        qx = matmul([rmsnorm(h, norm_x_g[l])], wq_x[l])
        a_p = mem_attention_prompt(qx, mk, mv, b, L, M, MEM_HEADS)
        a_s = mem_attention_sample(qx[n_p:].reshape(db, dn, MEM_HEADS, D // MEM_HEADS),
                                   cache_mem_k[l:l + 1], cache_mem_v[l:l + 1])
        att = jnp.concatenate([a_p, a_s.reshape(n_s, D).astype(BF16)], axis=0)
        h = matmul([att], wo_x[l], res=h)

        n3 = rmsnorm(h, norm_ffn_g[l])
        gates = peer_router(matmul([n3], peer_wq[l]), peer_subkeys[l])
        ffn = peer_experts(n3, gates, cast(peer_u[l], BF16), cast(peer_v[l], BF16))
        if l + 1 < depth:
            h = add_rmsnorm(h, ffn, None)
    y_p, y_s = add_rmsnorm_split(h, ffn, norm_final_g, n_p)
    y_prompt, y_sample = y_p.reshape(b, L, D), y_s.reshape(db, dn, D)
    st = lambda xs: jnp.stack(xs)
    return (y_prompt, y_sample, st(hgp), st(wkp[0]), st(wvp[0]), st(wkp[1]), st(wvp[1]), st(wkp[2]), st(wvp[2]),
            st(mkp), st(mvp), st(hgs), st(wks[0]), st(wvs[0]), st(wks[1]), st(wvs[1]), st(wks[2]), st(wvs[2]))
```

```python
import functools
import math

import jax
import jax.numpy as jnp
from jax import lax
from jax.experimental import pallas as pl
from jax.experimental.pallas import tpu as pltpu

F32 = jnp.float32
BF16 = jnp.bfloat16
NEG_INF = -1e30
RMS_EPS = 1e-6
HEAD_DIM = 128
HG_CHUNK = 16
DIL_PATTERNS = ((128, 1), (512, 4), (2048, 16))
ALIBI_MAX = 8.0
MEM_HEADS = 4
PK_HEADS = 8
N_KEYS = 128
PK_TOPK = 16

VMEM_LIMIT_BYTES = 56 * 1024 * 1024


def _params(*sem):
    return pltpu.CompilerParams(dimension_semantics=sem, vmem_limit_bytes=VMEM_LIMIT_BYTES)


def _row_tile(n, target):
    best = None
    for t in range(8, min(n, target) + 1, 8):
        if n % t == 0:
            best = t
    assert best is not None, (n, target)
    return best


def _rmsnorm_kernel(x_ref, g_ref, o_ref):
    x = x_ref[...]
    ms = jnp.mean(x * x, axis=-1, keepdims=True)
    o_ref[...] = (x * lax.rsqrt(ms + RMS_EPS) * g_ref[...]).astype(o_ref.dtype)


def rmsnorm(x, g, out_dtype=BF16, tm=512):
    T, D = x.shape
    tm = _row_tile(T, tm)
    return pl.pallas_call(
        _rmsnorm_kernel,
        out_shape=jax.ShapeDtypeStruct((T, D), out_dtype),
        grid=(T // tm,),
        in_specs=[pl.BlockSpec((tm, D), lambda i: (i, 0)), pl.BlockSpec((1, D), lambda i: (0, 0))],
        out_specs=pl.BlockSpec((tm, D), lambda i: (i, 0)),
        compiler_params=_params("parallel"),
        name="rmsnorm",
    )(x, g.reshape(1, D).astype(F32))


def _rmsnorm_cat_kernel(xa_ref, xb_ref, g_ref, n_ref, h_ref, *, blocks_a):
    def emit(x):
        h_ref[...] = x
        ms = jnp.mean(x * x, axis=-1, keepdims=True)
        n_ref[...] = (x * lax.rsqrt(ms + RMS_EPS) * g_ref[...]).astype(n_ref.dtype)

    @pl.when(pl.program_id(0) < blocks_a)
    def _():
        emit(xa_ref[...])

    @pl.when(pl.program_id(0) >= blocks_a)
    def _():
        emit(xb_ref[...])


def rmsnorm_cat(xa, xb, g, tm=256):
    (na, D), nb = xa.shape, xb.shape[0]
    tm = _row_tile(math.gcd(na, nb), tm)
    ba, T = na // tm, na + nb
    out = pl.BlockSpec((tm, D), lambda i: (i, 0))
    return pl.pallas_call(
        functools.partial(_rmsnorm_cat_kernel, blocks_a=ba),
        out_shape=(jax.ShapeDtypeStruct((T, D), BF16), jax.ShapeDtypeStruct((T, D), F32)),
        grid=(T // tm,),
        in_specs=[pl.BlockSpec((tm, D), lambda i: (jnp.minimum(i, ba - 1), 0)),
                  pl.BlockSpec((tm, D), lambda i: (jnp.maximum(i - ba, 0), 0)),
                  pl.BlockSpec((1, D), lambda i: (0, 0))],
        out_specs=(out, out),
        compiler_params=_params("arbitrary"),
        name="rmsnorm_cat",
    )(xa, xb, g.reshape(1, D).astype(F32))


def _add_rmsnorm_split_kernel(x_ref, y_ref, g_ref, oa_ref, ob_ref, *, blocks_a):
    x = x_ref[...] + y_ref[...]
    ms = jnp.mean(x * x, axis=-1, keepdims=True)
    x = x * lax.rsqrt(ms + RMS_EPS) * g_ref[...]

    @pl.when(pl.program_id(0) < blocks_a)
    def _():
        oa_ref[...] = x

    @pl.when(pl.program_id(0) >= blocks_a)
    def _():
        ob_ref[...] = x


def add_rmsnorm_split(x, y, g, na, tm=256):
    T, D = x.shape
    tm = _row_tile(math.gcd(na, T - na), tm)
    ba = na // tm
    blk = pl.BlockSpec((tm, D), lambda i: (i, 0))
    return pl.pallas_call(
        functools.partial(_add_rmsnorm_split_kernel, blocks_a=ba),
        out_shape=(jax.ShapeDtypeStruct((na, D), F32), jax.ShapeDtypeStruct((T - na, D), F32)),
        grid=(T // tm,),
        in_specs=[blk, blk, pl.BlockSpec((1, D), lambda i: (0, 0))],
        out_specs=(pl.BlockSpec((tm, D), lambda i: (jnp.minimum(i, ba - 1), 0)),
                   pl.BlockSpec((tm, D), lambda i: (jnp.maximum(i - ba, 0), 0))),
        compiler_params=_params("arbitrary"),
        name="add_rmsnorm_split",
    )(x, y, g.reshape(1, D).astype(F32))


def _add_rmsnorm_kernel(x_ref, y_ref, g_ref, o_ref, *, normalize):
    x = x_ref[...] + y_ref[...]
    if normalize:
        ms = jnp.mean(x * x, axis=-1, keepdims=True)
        x = x * lax.rsqrt(ms + RMS_EPS) * g_ref[...]
    o_ref[...] = x.astype(o_ref.dtype)


def add_rmsnorm(x, y, g, tm=256):
    T, D = x.shape
    tm = _row_tile(T, tm)
    gain = jnp.ones((1, D), F32) if g is None else g.reshape(1, D).astype(F32)
    return pl.pallas_call(
        functools.partial(_add_rmsnorm_kernel, normalize=g is not None),
        out_shape=jax.ShapeDtypeStruct((T, D), F32),
        grid=(T // tm,),
        in_specs=[pl.BlockSpec((tm, D), lambda i: (i, 0)), pl.BlockSpec((tm, D), lambda i: (i, 0)),
                  pl.BlockSpec((1, D), lambda i: (0, 0))],
        out_specs=pl.BlockSpec((tm, D), lambda i: (i, 0)),
        compiler_params=_params("parallel"),
        name="add_rmsnorm",
    )(x, y, gain)


def _mm_kernel(*refs, k_splits, has_res):
    n_a = len(k_splits)
    a_refs, w_ref = refs[:n_a], refs[n_a]
    res_ref = refs[n_a + 1] if has_res else None
    o_ref = refs[-1]
    acc, off = None, 0
    for a_ref, kk in zip(a_refs, k_splits):
        part = jnp.dot(a_ref[...], w_ref[off:off + kk, :].astype(BF16), preferred_element_type=F32)
        acc = part if acc is None else acc + part
        off += kk
    if has_res:
        acc = acc + res_ref[...]
    o_ref[...] = acc.astype(o_ref.dtype)


def matmul(a_list, w, res=None, out_dtype=F32, tm=1088, tn=512):
    M = a_list[0].shape[0]
    k_splits = tuple(a.shape[1] for a in a_list)
    K, N = w.shape
    assert sum(k_splits) == K
    tm, tn = _row_tile(M, tm), min(tn, N)
    assert N % tn == 0
    in_specs = [pl.BlockSpec((tm, kk), lambda i, j: (i, 0)) for kk in k_splits]
    in_specs.append(pl.BlockSpec((K, tn), lambda i, j: (0, j)))
    args = list(a_list) + [w]
    if res is not None:
        in_specs.append(pl.BlockSpec((tm, tn), lambda i, j: (i, j)))
        args.append(res)
    return pl.pallas_call(
        functools.partial(_mm_kernel, k_splits=k_splits, has_res=res is not None),
        out_shape=jax.ShapeDtypeStruct((M, N), out_dtype),
        grid=(M // tm, N // tn),
        in_specs=in_specs,
        out_specs=pl.BlockSpec((tm, tn), lambda i, j: (i, j)),
        compiler_params=_params("parallel", "arbitrary"),
        name="matmul",
    )(*args)


def _silu(x):
    return x * jax.nn.sigmoid(x)


def _split3(x):
    hi = x.astype(BF16)
    r1 = x - hi.astype(F32)
    mid = r1.astype(BF16)
    lo = (r1 - mid.astype(F32)).astype(BF16)
    return hi, mid, lo


def _rows_bcast(x, row_ids, reps):
    W = x.shape[1]
    parts = []
    for r in row_ids:
        row = jnp.zeros((1, W), x.dtype) if r < 0 else x[r:r + 1, :]
        parts.append(jnp.broadcast_to(row, (reps, W)))
    return parts[0] if len(parts) == 1 else jnp.concatenate(parts, axis=0)


def _dot_nt(a, b):
    return lax.dot_general(a, b, (((1,), (1,)), ((), ())), preferred_element_type=F32)


def _dot_tn(a, b):
    return lax.dot_general(a, b, (((0,), (0,)), ((), ())), preferred_element_type=F32)


def _hgrn_chunk(q, k, g, v, st, leaf):
    C = q.shape[0]
    row = lax.broadcasted_iota(jnp.int32, (C, q.shape[1]), 0)
    it = lax.broadcasted_iota(jnp.int32, (C, C), 0)
    js = lax.broadcasted_iota(jnp.int32, (C, C), 1)
    tri = jnp.where(it >= js, 1.0, 0.0).astype(BF16)
    g_hi, g_mid, g_lo = _split3(g)
    cum = (jnp.dot(tri, g_hi, preferred_element_type=F32) + jnp.dot(tri, g_mid, preferred_element_type=F32)
           + jnp.dot(tri, g_lo, preferred_element_type=F32))
    last = cum[C - 1:C, :]

    a = jnp.zeros((C, C), F32)
    bs = C
    while bs > leaf:
        half = bs // 2
        upper = (row % bs) >= half
        cmid = _rows_bcast(cum, [b0 + half - 1 for b0 in range(0, C, bs)], bs)
        e = jnp.exp(jnp.where(upper, cum - cmid, cmid - cum))
        q_l = jnp.where(upper, q * e, 0.0).astype(BF16)
        k_l = jnp.where(upper, 0.0, k * e).astype(BF16)
        a = a + jnp.where((it // bs) == (js // bs), _dot_nt(q_l, k_l), 0.0)
        bs = half
    cstart = _rows_bcast(cum, [b0 - 1 for b0 in range(0, C, leaf)], leaf)
    q_f = (q * jnp.exp(cum - cstart)).astype(BF16)
    k_f = (k * jnp.exp(cstart - cum)).astype(BF16)
    a = a + jnp.where(((it // leaf) == (js // leaf)) & (it >= js), _dot_nt(q_f, k_f), 0.0)

    vb = v.astype(BF16)
    q_g = (q * jnp.exp(cum)).astype(BF16)
    o = jnp.dot(a.astype(BF16), vb, preferred_element_type=F32) + _dot_nt(q_g, st.astype(BF16))
    k_e = (k * jnp.exp(last - cum)).astype(BF16)
    st_new = st * jnp.exp(last) + _dot_tn(vb, k_e)
    return o, st_new


def _hgrn_gates(hq, hf, lb):
    q = _silu(hq)
    f = lb + (1.0 - lb) * jax.nn.sigmoid(hf)
    return q, 1.0 - f, jnp.log(f)


def _hgrn_out(o, hgate, ng):
    ms = jnp.mean(o * o, axis=-1, keepdims=True)
    return o * lax.rsqrt(ms + RMS_EPS) * ng * _silu(hgate)


def _hgrn_prompt_kernel(hq_ref, hf_ref, hi_ref, hg_ref, lb_ref, ng_ref, o_ref, s_ref, st_scr, *, chunk, leaf):
    c = pl.program_id(2)

    @pl.when(c == 0)
    def _():
        st_scr[...] = jnp.zeros_like(st_scr)

    rows, width = hq_ref.shape
    ng = ng_ref[...]
    for s0 in range(0, rows, chunk):
        sl = slice(s0, s0 + chunk)
        for hh in range(width // HEAD_DIM):
            cs = slice(hh * HEAD_DIM, (hh + 1) * HEAD_DIM)
            q, k, g = _hgrn_gates(hq_ref[sl, cs], hf_ref[sl, cs], lb_ref[:, cs])
            o, st_new = _hgrn_chunk(q, k, g, hi_ref[sl, cs], st_scr[hh], leaf)
            st_scr[hh] = st_new
            o_ref[sl, cs] = _hgrn_out(o, hg_ref[sl, cs], ng).astype(o_ref.dtype)

    @pl.when(c == pl.num_programs(2) - 1)
    def _():
        for hh in range(width // HEAD_DIM):
            s_ref[0, hh] = st_scr[hh].T


def hgrn_prompt(proj, lb, ng, b, L, n_heads, col_blocks, rows_per_step=512, chunk=128, leaf=HG_CHUNK,
                heads_per_step=4):
    rows = min(rows_per_step, L)
    hps = heads_per_step
    assert L % rows == 0 and rows % chunk == 0 and n_heads % hps == 0 and all(cb % hps == 0 for cb in col_blocks)
    nc = L // rows
    width = hps * HEAD_DIM

    def in_spec(cb):
        return pl.BlockSpec((rows, width), lambda bi, h, c: (bi * nc + c, cb // hps + h))

    return pl.pallas_call(
        functools.partial(_hgrn_prompt_kernel, chunk=chunk, leaf=leaf),
        out_shape=(jax.ShapeDtypeStruct((b * L, n_heads * HEAD_DIM), BF16),
                   jax.ShapeDtypeStruct((b, n_heads, HEAD_DIM, HEAD_DIM), F32)),
        grid=(b, n_heads // hps, nc),
        in_specs=[in_spec(cb) for cb in col_blocks] + [
            pl.BlockSpec((1, width), lambda bi, h, c: (0, h)),
            pl.BlockSpec((1, HEAD_DIM), lambda bi, h, c: (0, 0))],
        out_specs=(pl.BlockSpec((rows, width), lambda bi, h, c: (bi * nc + c, h)),
                   pl.BlockSpec((1, hps, HEAD_DIM, HEAD_DIM), lambda bi, h, c: (bi, h, 0, 0))),
        scratch_shapes=[pltpu.VMEM((hps, HEAD_DIM, HEAD_DIM), F32)],
        compiler_params=_params("parallel", "parallel", "arbitrary"),
        name="hgrn_prompt",
    )(proj, proj, proj, proj, lb, ng)


def _alibi_slopes(n_pat, n_slots):
    k = jnp.arange(1, n_pat * n_slots + 1, dtype=F32)
    return (2.0 ** (-ALIBI_MAX * k / (n_pat * n_slots))).reshape(n_pat, n_slots)


def _strided_rows(ref, start, size, stride):
    if stride == 1:
        return ref[pl.ds(start, size), :]
    return ref[pl.ds(start, size, stride=stride), :]


def _band_softmax(q, kk, vv, bias, valid, scale):
    s = _dot_nt(q.astype(BF16), kk.astype(BF16)) * scale - bias
    s = jnp.where(valid, s, NEG_INF)
    m = jnp.max(s, axis=-1, keepdims=True)
    p = jnp.exp(s - m)
    l = jnp.sum(p, axis=-1, keepdims=True)
    o = jnp.dot(p.astype(BF16), vv.astype(BF16), preferred_element_type=F32) / l
    return o, m + jnp.log(l)


def _dil_attn_kernel(q_ref, kp_ref, kc_ref, vp_ref, vc_ref, slope_ref, o_ref, lse_ref, *, dil, band, units):
    first = pl.program_id(2) == 0
    span = band * dil
    scale = HEAD_DIM ** -0.5
    qi = lax.broadcasted_iota(jnp.int32, (band, 2 * band), 0)
    kj = lax.broadcasted_iota(jnp.int32, (band, 2 * band), 1)
    off = qi + band - kj
    in_band = (off >= 0) & (off <= band)
    bias = slope_ref[0] * float(dil) * off.astype(F32)
    for u in range(units):
        for r in range(dil):
            base = u * span + r
            q = _strided_rows(q_ref, base, band, dil)
            if u == 0:
                kp, vp = _strided_rows(kp_ref, r, band, dil), _strided_rows(vp_ref, r, band, dil)
                valid = in_band & (jnp.logical_not(first) | (kj >= band))
            else:
                kp, vp = _strided_rows(kc_ref, base - span, band, dil), _strided_rows(vc_ref, base - span, band, dil)
                valid = in_band
            kk = jnp.concatenate([kp, _strided_rows(kc_ref, base, band, dil)], axis=0)
            vv = jnp.concatenate([vp, _strided_rows(vc_ref, base, band, dil)], axis=0)
            o, lse = _band_softmax(q, kk, vv, bias, valid, scale)
            lse_b = jnp.broadcast_to(lse, (band, HEAD_DIM))
            if dil == 1:
                o_ref[pl.ds(base, band), :] = o
                lse_ref[pl.ds(base, band), :] = lse_b
            else:
                o_ref[pl.ds(base, band, stride=dil), :] = o
                lse_ref[pl.ds(base, band, stride=dil), :] = lse_b


def dilated_attention_prompt(proj, slopes_p, b, L, n_slots, qcol, kcol, vcol, win, dil, units):
    band = win // dil
    span = band * dil
    rows = span * units
    assert L % rows == 0
    nblk = L // rows
    cur = lambda c0: pl.BlockSpec((rows, HEAD_DIM), lambda bi, h, i: (bi * nblk + i, c0 + h))
    prev = lambda c0: pl.BlockSpec(
        (span, HEAD_DIM), lambda bi, h, i: (bi * nblk * units + jnp.maximum(i * units - 1, 0), c0 + h))
    out = pl.BlockSpec((rows, HEAD_DIM), lambda bi, h, i: (bi * nblk + i, h))
    shape = jax.ShapeDtypeStruct((b * L, n_slots * HEAD_DIM), F32)
    slope_rep = jnp.broadcast_to(slopes_p.reshape(n_slots, 1, 1), (n_slots, 1, 2 * band)).astype(F32)
    return pl.pallas_call(
        functools.partial(_dil_attn_kernel, dil=dil, band=band, units=units),
        out_shape=(shape, shape),
        grid=(b, n_slots, nblk),
        in_specs=[cur(qcol), prev(kcol), cur(kcol), prev(vcol), cur(vcol),
                  pl.BlockSpec((1, 1, 2 * band), lambda bi, h, i: (h, 0, 0))],
        out_specs=(out, out),
        compiler_params=_params("parallel", "parallel", "arbitrary"),
        name=f"dil_attn_prompt_d{dil}",
    )(proj, proj, proj, proj, proj, slope_rep)


def _merge_kernel(*refs):
    n = (len(refs) - 1) // 2
    o_refs, l_refs, out_ref = refs[:n], refs[n:2 * n], refs[-1]
    ls = [r[...] for r in l_refs]
    m = functools.reduce(jnp.maximum, ls)
    ws = [jnp.exp(l - m) for l in ls]
    num = functools.reduce(lambda a, c: a + c, [w * r[...] for w, r in zip(ws, o_refs)])
    den = functools.reduce(lambda a, c: a + c, ws)
    out_ref[...] = (num / den).astype(out_ref.dtype)


def merge_patterns(outs, lses, tm=512):
    T, W = outs[0].shape
    tm = _row_tile(T, tm)
    spec = pl.BlockSpec((tm, W), lambda i: (i, 0))
    return pl.pallas_call(
        _merge_kernel,
        out_shape=jax.ShapeDtypeStruct((T, W), BF16),
        grid=(T // tm,),
        in_specs=[spec] * (2 * len(outs)),
        out_specs=spec,
        compiler_params=_params("parallel"),
        name="merge_patterns",
    )(*outs, *lses)


def _mem_attn_kernel(q_ref, k_ref, v_ref, o_ref):
    scale = q_ref.shape[-1] ** -0.5
    s = _dot_nt(q_ref[...].astype(BF16), k_ref[...].astype(BF16)) * scale
    m = jnp.max(s, axis=-1, keepdims=True)
    p = jnp.exp(s - m)
    l = jnp.sum(p, axis=-1, keepdims=True)
    o = jnp.dot(p.astype(BF16), v_ref[...].astype(BF16), preferred_element_type=F32) / l
    o_ref[...] = o.astype(o_ref.dtype)


def mem_attention_prompt(q, mk, mv, b, L, M, n_heads, tq=1024):
    D = mk.shape[1]
    hd = D // n_heads
    tq = _row_tile(L, tq)
    nq = L // tq
    return pl.pallas_call(
        _mem_attn_kernel,
        out_shape=jax.ShapeDtypeStruct((b * L, D), BF16),
        grid=(b, n_heads, nq),
        in_specs=[pl.BlockSpec((tq, hd), lambda bi, h, i: (bi * nq + i, h)),
                  pl.BlockSpec((M, hd), lambda bi, h, i: (bi, h)),
                  pl.BlockSpec((M, hd), lambda bi, h, i: (bi, h))],
        out_specs=pl.BlockSpec((tq, hd), lambda bi, h, i: (bi * nq + i, h)),
        compiler_params=_params("parallel", "parallel", "arbitrary"),
        name="mem_attn_prompt",
    )(q, mk, mv)


def _dot_nt_x3(a, b):
    a_hi, b_hi = a.astype(BF16), b.astype(BF16)
    a_lo, b_lo = (a - a_hi.astype(F32)).astype(BF16), (b - b_hi.astype(F32)).astype(BF16)
    return _dot_nt(a_hi, b_hi) + _dot_nt(a_hi, b_lo) + _dot_nt(a_lo, b_hi)


def _topk_rows(x, k, payload=None):
    n = x.shape[0]
    iota = lax.broadcasted_iota(jnp.int32, x.shape, 0).astype(F32)
    vals, outs = [], []
    for _ in range(k):
        m = jnp.max(x, axis=0, keepdims=True)
        pos = jnp.min(jnp.where(x == m, iota, float(n)), axis=0, keepdims=True)
        sel = iota == pos
        vals.append(m)
        outs.append(pos if payload is None else jnp.max(jnp.where(sel, payload, -1.0), axis=0, keepdims=True))
        x = jnp.where(sel, -jnp.inf, x)
    return jnp.concatenate(vals, axis=0), jnp.concatenate(outs, axis=0)


def _candidates(v1, i1, v2, i2, kk):
    vals, idxs, a = [], [], 0
    while a < kk and kk // (a + 1) >= 2:
        nb = min(kk, -(-(kk // (a + 1)) // 8) * 8)
        vals.append(v1[a:a + 1, :] + v2[:nb, :])
        idxs.append(i1[a:a + 1, :] * float(N_KEYS) + i2[:nb, :])
        a += 1
    if a < kk:
        vals.append(v1[a:, :] + v2[0:1, :])
        idxs.append(i1[a:, :] * float(N_KEYS) + i2[0:1, :])
    return jnp.concatenate(vals, axis=0), jnp.concatenate(idxs, axis=0)


_STAGE_PITCH = N_KEYS + 8


def _router_kernel(qp_ref, sk_ref, g_ref, ii_scr, ij_scr, w_scr, iit_scr, ijt_scr, wt_scr, stage_scr):
    tb = qp_ref.shape[0]
    kk = PK_TOPK
    for h in range(PK_HEADS):
        tops = []
        for c in range(2):
            col = (2 * h + c) * HEAD_DIM
            sc = _dot_nt_x3(sk_ref[2 * h + c], qp_ref[:, col:col + HEAD_DIM])
            tops.append(_topk_rows(sc, kk))
        (v1, i1), (v2, i2) = tops
        cand, cidx = _candidates(v1, i1, v2, i2, kk)
        best, eidx = _topk_rows(cand, kk, payload=cidx)
        e = jnp.exp(best - best[0:1, :])
        gate = e / jnp.sum(e, axis=0, keepdims=True)
        ei = jnp.floor(eidx * (1.0 / N_KEYS))
        ii_scr[h * kk:(h + 1) * kk, :] = ei
        ij_scr[h * kk:(h + 1) * kk, :] = eidx - ei * float(N_KEYS)
        w_scr[h * kk:(h + 1) * kk, :] = gate
    iit_scr[...] = ii_scr[...].T
    ijt_scr[...] = ij_scr[...].T
    wt_scr[...] = w_scr[...].T
    sub = lax.broadcasted_iota(jnp.int32, (N_KEYS, PK_HEADS * kk), 0).astype(F32)

    def per_token(t, carry):
        a = jnp.where(sub == iit_scr[pl.ds(t, 1), :], wt_scr[pl.ds(t, 1), :], 0.0).astype(BF16)
        bsel = jnp.where(sub == ijt_scr[pl.ds(t, 1), :], 1.0, 0.0).astype(BF16)
        row0 = pl.multiple_of(t * _STAGE_PITCH, 8)
        stage_scr[pl.ds(row0, N_KEYS), :] = _dot_nt(a, bsel)
        return carry

    lax.fori_loop(0, tb, per_token, 0, unroll=8)
    for i in range(N_KEYS):
        g_ref[:, i * N_KEYS:(i + 1) * N_KEYS] = stage_scr[pl.ds(i, tb, stride=_STAGE_PITCH), :].astype(g_ref.dtype)


def peer_router(qp, subkeys, tb=128):
    T = qp.shape[0]
    assert T % tb == 0 and tb == N_KEYS
    sk = subkeys.reshape(PK_HEADS * 2, N_KEYS, HEAD_DIM)
    slots = PK_HEADS * PK_TOPK
    return pl.pallas_call(
        _router_kernel,
        out_shape=jax.ShapeDtypeStruct((T, N_KEYS * N_KEYS), BF16),
        grid=(T // tb,),
        in_specs=[pl.BlockSpec((tb, qp.shape[1]), lambda i: (i, 0)),
                  pl.BlockSpec(sk.shape, lambda i: (0, 0, 0))],
        out_specs=pl.BlockSpec((tb, N_KEYS * N_KEYS), lambda i: (i, 0)),
        scratch_shapes=[pltpu.VMEM((slots, tb), F32)] * 3 + [pltpu.VMEM((tb, slots), F32)] * 3
        + [pltpu.VMEM((tb * _STAGE_PITCH, N_KEYS), F32)],
        compiler_params=_params("parallel"),
        name="peer_router",
    )(qp, sk)


def _gelu(x):
    return 0.5 * x * (1.0 + lax.erf(x * (2.0 ** -0.5)))


def _cast_kernel(x_ref, o_ref):
    o_ref[...] = x_ref[...].astype(o_ref.dtype)


def cast(x, dtype, tm=512):
    R, C = x.shape
    tm = _row_tile(R, tm)
    spec = pl.BlockSpec((tm, C), lambda i: (i, 0))
    return pl.pallas_call(
        _cast_kernel, out_shape=jax.ShapeDtypeStruct((R, C), dtype), grid=(R // tm,), in_specs=[spec],
        out_specs=spec, compiler_params=_params("parallel"), name="cast",
    )(x)


def _peer_kernel(n_ref, g_ref, u_ref, v_ref, o_ref, hid_a, hid_b):
    j = pl.program_id(1)

    @pl.when(j == 0)
    def _():
        hid_b[...] = jnp.zeros_like(hid_b)
        o_ref[...] = jnp.zeros_like(o_ref)

    def step(prev_scr, next_scr):
        part = jnp.dot(prev_scr[...], v_ref[...].astype(BF16), preferred_element_type=F32)
        xu = _dot_nt(n_ref[...], u_ref[...].astype(BF16))
        next_scr[...] = (g_ref[...].astype(F32) * _gelu(xu)).astype(BF16)
        o_ref[...] += part

    @pl.when(j % 2 == 0)
    def _():
        step(hid_b, hid_a)

    @pl.when(j % 2 == 1)
    def _():
        step(hid_a, hid_b)


def peer_experts(n, gates, u, v, tm=1088, te=256):
    T, D = n.shape
    E = u.shape[0]
    tm = _row_tile(T, tm)
    assert E % te == 0
    ne = E // te
    resident = pl.Buffered(1)
    return pl.pallas_call(
        _peer_kernel,
        out_shape=jax.ShapeDtypeStruct((T, D), F32),
        grid=(T // tm, ne + 1),
        in_specs=[pl.BlockSpec((tm, D), lambda i, j: (i, 0), pipeline_mode=resident),
                  pl.BlockSpec((tm, te), lambda i, j: (i, jnp.minimum(j, ne - 1))),
                  pl.BlockSpec((te, D), lambda i, j: (jnp.minimum(j, ne - 1), 0)),
                  pl.BlockSpec((te, D), lambda i, j: (jnp.maximum(j - 1, 0), 0))],
        out_specs=pl.BlockSpec((tm, D), lambda i, j: (i, 0), pipeline_mode=resident),
        scratch_shapes=[pltpu.VMEM((tm, te), BF16)] * 2,
        compiler_params=_params("parallel", "arbitrary"),
        name="peer_experts",
    )(n, gates, u, v)


def _hgrn_sample_kernel(hq_ref, hf_ref, hi_ref, hg_ref, lb_ref, ng_ref, s_ref, o_ref, so_ref, *, n_tok):
    rows, width = hq_ref.shape
    n_seq, n_head = rows // n_tok, width // HEAD_DIM
    q, k, g = _hgrn_gates(hq_ref[...], hf_ref[...], lb_ref[...])
    t = lax.broadcasted_iota(jnp.int32, (rows, width), 0) % n_tok
    cum, sh = g, 1
    while sh < n_tok:
        cum = cum + jnp.where(t >= sh, pltpu.roll(cum, sh, axis=0), 0.0)
        sh *= 2
    last = _rows_bcast(cum, [s * n_tok + n_tok - 1 for s in range(n_seq)], n_tok)
    q_in = q * jnp.exp(cum)
    k_in = k * jnp.exp(-cum)
    k_out = k * jnp.exp(last - cum)
    dec_t = jnp.exp(last).T
    v = hi_ref[...]
    it = lax.broadcasted_iota(jnp.int32, (n_tok, n_tok), 0)
    js = lax.broadcasted_iota(jnp.int32, (n_tok, n_tok), 1)
    ng = ng_ref[...]
    for s in range(n_seq):
        r = slice(s * n_tok, (s + 1) * n_tok)
        for h in range(n_head):
            c = slice(h * HEAD_DIM, (h + 1) * HEAD_DIM)
            st = s_ref[0, s, h]
            qh, vh = q_in[r, c].astype(BF16), v[r, c].astype(BF16)
            a = jnp.where(it >= js, _dot_nt(qh, k_in[r, c].astype(BF16)), 0.0)
            o = jnp.dot(a.astype(BF16), vh, preferred_element_type=F32)
            o = o + jnp.dot(qh, st.astype(BF16), preferred_element_type=F32)
            dec = dec_t[c, s * n_tok:s * n_tok + 1]
            so_ref[0, s, h] = st * dec + _dot_tn(k_out[r, c].astype(BF16), vh)
            o_ref[r, c] = _hgrn_out(o, hg_ref[r, c], ng).astype(o_ref.dtype)


def hgrn_sample(proj, row0, state, lb, ng, n_tok, col_blocks, seqs_per_step=4, heads_per_step=4):
    _, n_seq, n_heads, _, _ = state.shape
    rows, width = seqs_per_step * n_tok, heads_per_step * HEAD_DIM
    assert n_seq % seqs_per_step == 0 and n_heads % heads_per_step == 0 and row0 % rows == 0
    r0, nhb = row0 // rows, n_heads // heads_per_step

    def in_spec(cb):
        return pl.BlockSpec((rows, width), lambda i, j: (r0 + i, cb // heads_per_step + j))

    st_spec = pl.BlockSpec((1, seqs_per_step, heads_per_step, HEAD_DIM, HEAD_DIM), lambda i, j: (0, i, j, 0, 0))
    return pl.pallas_call(
        functools.partial(_hgrn_sample_kernel, n_tok=n_tok),
        out_shape=(jax.ShapeDtypeStruct((n_seq * n_tok, n_heads * HEAD_DIM), BF16),
                   jax.ShapeDtypeStruct(state.shape, F32)),
        grid=(n_seq // seqs_per_step, nhb),
        in_specs=[in_spec(cb) for cb in col_blocks] + [
            pl.BlockSpec((1, width), lambda i, j: (0, j)),
            pl.BlockSpec((1, HEAD_DIM), lambda i, j: (0, 0)),
            st_spec],
        out_specs=(pl.BlockSpec((rows, width), lambda i, j: (i, j)), st_spec),
        compiler_params=_params("parallel", "parallel"),
        name="hgrn_sample",
    )(proj, proj, proj, proj, lb, ng, state)


def _lane_sum_rep(x):
    ones = jnp.ones((x.shape[1], HEAD_DIM), BF16)
    hi = x.astype(BF16)
    lo = (x - hi.astype(F32)).astype(BF16)
    return jnp.dot(hi, ones, preferred_element_type=F32) + jnp.dot(lo, ones, preferred_element_type=F32)


def _dil_attn_sample_kernel(q_ref, kn_ref, vn_ref, ck_ref, cv_ref, ckx_ref, cvx_ref, slope_ref,
                            o_ref, lse_ref, ok_ref, ov_ref, m_scr, l_scr, acc_scr, *, dil, band):
    n, H, D = q_ref.shape[1:]
    rc = ck_ref.shape[2]
    c, nc = pl.program_id(1), pl.num_programs(1)
    scale = D ** -0.5
    slope = slope_ref[...] * float(dil)

    last = c == nc - 1
    for src, nxt, new, dst in ((ck_ref, ckx_ref, kn_ref, ok_ref), (cv_ref, cvx_ref, vn_ref, ov_ref)):
        dst[0, 0, 0:rc - n] = src[0, 0, n:rc]
        dst[0, 0, rc - n:rc] = jnp.where(last, new[0], nxt[0, 0])

    @pl.when(c == 0)
    def _():
        for i in range(n):
            q = q_ref[0, i]
            news = [ip for ip in range(i + 1) if (i - ip) % dil == 0]
            s_n = [_lane_sum_rep(kn_ref[0, ip] * q) * scale - slope * float((i - ip) // dil) for ip in news]
            m = functools.reduce(jnp.maximum, s_n)
            p_n = [jnp.exp(s - m) for s in s_n]
            m_scr[i] = m
            l_scr[i] = functools.reduce(lambda a, b: a + b, p_n)
            acc_scr[i] = functools.reduce(lambda a, b: a + b, [p * vn_ref[0, ip] for p, ip in zip(p_n, news)])

    rows = rc // dil if dil >= n else band
    mm = lax.broadcasted_iota(jnp.int32, (rows, H, D), 0)
    for i in range(n):
        q = q_ref[0, i]
        if dil >= n:
            kc = ck_ref[0, 0, pl.ds(i, rows, stride=dil)]
            vc = cv_ref[0, 0, pl.ds(i, rows, stride=dil)]
            jc = (nc - c) * rows - mm
            ok_c = None
        else:
            kc = ck_ref[0, 0, pl.ds(rc - band, band)]
            vc = cv_ref[0, 0, pl.ds(rc - band, band)]
            jc = band + i - mm
            ok_c = jc <= band
        s_c = _lane_sum_rep((kc * q[None]).reshape(rows * H, D)).reshape(rows, H, D) * scale
        s_c = s_c - slope[None] * jc.astype(F32)
        if ok_c is not None:
            s_c = jnp.where(ok_c, s_c, NEG_INF)
        m_old = m_scr[i]
        m_new = jnp.maximum(m_old, jnp.max(s_c, axis=0))
        alpha = jnp.exp(m_old - m_new)
        p_c = jnp.exp(s_c - m_new[None])
        l_scr[i] = l_scr[i] * alpha + jnp.sum(p_c, axis=0)
        acc_scr[i] = acc_scr[i] * alpha + jnp.sum(p_c * vc, axis=0)
        m_scr[i] = m_new

    @pl.when(last)
    def _():
        for i in range(n):
            o_ref[0, i] = acc_scr[i] / l_scr[i]
            lse_ref[0, i] = m_scr[i] + jnp.log(l_scr[i])


def dilated_attention_sample(q, k_new, v_new, cache_k, cache_v, slopes_p, win, dil, chunk_rows=512):
    B, n, H, D = q.shape
    keep = cache_k.shape[2]
    band = win // dil
    assert keep == band * dil and keep > n and keep % n == 0, "window cache must hold exactly one window"
    rc = keep if dil < n else min(keep, chunk_rows)
    assert keep % rc == 0 and rc % dil == 0 and rc % n == 0 and (dil >= n or dil == 1)
    nc = keep // rc
    new_spec = pl.BlockSpec((1, n, H, D), lambda b, c: (b, 0, 0, 0))
    c_spec = pl.BlockSpec((1, 1, rc, H, D), lambda b, c: (0, b, c, 0, 0))
    x_spec = pl.BlockSpec((1, 1, n, H, D), lambda b, c: (0, b, jnp.minimum((c + 1) * (rc // n), keep // n - 1), 0, 0))
    shape = jax.ShapeDtypeStruct((B, n, H, D), F32)
    slope_rep = jnp.broadcast_to(slopes_p.reshape(H, 1), (H, D)).astype(F32)
    return pl.pallas_call(
        functools.partial(_dil_attn_sample_kernel, dil=dil, band=band),
        out_shape=(shape, shape, jax.ShapeDtypeStruct(cache_k.shape, cache_k.dtype),
                   jax.ShapeDtypeStruct(cache_v.shape, cache_v.dtype)),
        grid=(B, nc),
        in_specs=[new_spec, new_spec, new_spec, c_spec, c_spec, x_spec, x_spec,
                  pl.BlockSpec((H, D), lambda b, c: (0, 0))],
        out_specs=(new_spec, new_spec, c_spec, c_spec),
        scratch_shapes=[pltpu.VMEM((n, H, D), F32)] * 3,
        compiler_params=_params("parallel", "arbitrary"),
        name=f"dil_attn_sample_d{dil}",
    )(q, k_new, v_new, cache_k, cache_v, cache_k, cache_v, slope_rep)


def _mem_attn_sample_kernel(q_ref, k_ref, v_ref, o_ref):
    n, H, D = q_ref.shape[1:]
    M = k_ref.shape[2]
    scale = D ** -0.5
    k2 = k_ref[0, 0].reshape(M * H, D).astype(BF16)
    v2 = v_ref[0, 0].reshape(M * H, D).astype(BF16)
    q2 = q_ref[0].reshape(n * H, D).astype(BF16)
    s = _dot_nt(q2, k2) * scale
    qh = lax.broadcasted_iota(jnp.int32, s.shape, 0) % H
    kh = lax.broadcasted_iota(jnp.int32, s.shape, 1) % H
    s = jnp.where(qh == kh, s, NEG_INF)
    m = jnp.max(s, axis=-1, keepdims=True)
    p = jnp.exp(s - m)
    l = jnp.sum(p, axis=-1, keepdims=True)
    o = jnp.dot(p.astype(BF16), v2, preferred_element_type=F32) / l
    o_ref[0] = o.reshape(n, H, D)


def mem_attention_sample(q, mem_k, mem_v):
    B, n, H, D = q.shape
    M = mem_k.shape[2]
    q_spec = pl.BlockSpec((1, n, H, D), lambda i: (i, 0, 0, 0))
    m_spec = pl.BlockSpec((1, 1, M, H, D), lambda i: (0, i, 0, 0, 0))
    return pl.pallas_call(
        _mem_attn_sample_kernel,
        out_shape=jax.ShapeDtypeStruct((B, n, H, D), F32),
        grid=(B,),
        in_specs=[q_spec, m_spec, m_spec],
        out_specs=q_spec,
        compiler_params=_params("parallel"),
        name="mem_attn_sample",
    )(q, mem_k, mem_v)


def kernel(x_prompt, x_sample, mem_prompt, state_hgrn, cache_w1_k, cache_w1_v, cache_w2_k, cache_w2_v, cache_w3_k, cache_w3_v, cache_mem_k, cache_mem_v, norm_mix_g, w_in, hg_lower_bound, hg_norm_g, w_out, norm_x_g, norm_mem_g, wq_x, wk_x, wv_x, wo_x, norm_ffn_g, peer_wq, peer_subkeys, peer_u, peer_v, norm_final_g):
    b, L, D = x_prompt.shape
    db, dn, _ = x_sample.shape
    depth = w_in.shape[0]
    n_p, n_s = b * L, db * dn
    hg_heads = state_hgrn.shape[2]
    n_slots = cache_w1_k.shape[3]
    n_pat = len(DIL_PATTERNS)
    M = mem_prompt.shape[1]
    hg_cols = (0, hg_heads, 2 * hg_heads, 3 * hg_heads)
    qcol, kcol, vcol = (4 * hg_heads + s * n_pat * n_slots for s in range(3))
    lbs = jnp.cumsum(jax.nn.softmax(hg_lower_bound.astype(F32), axis=0), axis=0)
    slopes = _alibi_slopes(n_pat, n_slots)
    cache_k = (cache_w1_k, cache_w2_k, cache_w3_k)
    cache_v = (cache_w1_v, cache_w2_v, cache_w3_v)
    prompt_units = (4, 1, 1)

    n1, h = rmsnorm_cat(x_prompt.reshape(n_p, D), x_sample.reshape(n_s, D), norm_mix_g[0])
    hgp, hgs, mkp, mvp = [], [], [], []
    wkp, wvp, wks, wvs = ([[] for _ in DIL_PATTERNS] for _ in range(4))
    for l in range(depth):
        lb, ng = lbs[l].reshape(1, -1), hg_norm_g[l].reshape(1, -1)
        proj = matmul([n1 if l == 0 else rmsnorm(h, norm_mix_g[l])], w_in[l])

        def head_cols(col0, p, rows):
            c0 = (col0 + p * n_slots) * HEAD_DIM
            return rows[:, c0:c0 + n_slots * HEAD_DIM]

        o_hg_p, s_p = hgrn_prompt(proj, lb, ng, b, L, hg_heads, hg_cols)
        outs, lses = [], []
        for p, (win, dil) in enumerate(DIL_PATTERNS):
            o, lse = dilated_attention_prompt(proj, slopes[p], b, L, n_slots, qcol + p * n_slots, kcol + p * n_slots,
                                              vcol + p * n_slots, win, dil, prompt_units[p])
            outs.append(o)
            lses.append(lse)
            keep = min(win, L)
            for col0, dst in ((kcol, wkp), (vcol, wvp)):
                tails = [head_cols(col0, p, proj[(bi + 1) * L - keep:(bi + 1) * L]) for bi in range(b)]
                dst[p].append(jnp.stack(tails).reshape(b, keep, n_slots, HEAD_DIM))
        o_at_p = merge_patterns(outs, lses)
        hgp.append(s_p)

        o_hg_s, s_s = hgrn_sample(proj, n_p, state_hgrn[l:l + 1], lb, ng, dn, hg_cols)
        hgs.append(s_s[0])
        proj_s = proj[n_p:]
        outs, lses = [], []
        for p, (win, dil) in enumerate(DIL_PATTERNS):
            q_s, k_s, v_s = (head_cols(c, p, proj_s).reshape(db, dn, n_slots, HEAD_DIM) for c in (qcol, kcol, vcol))
            ck, cv = cache_k[p][l:l + 1], cache_v[p][l:l + 1]
            o, lse, nk, nv = dilated_attention_sample(q_s, k_s, v_s, ck, cv, slopes[p], win, dil)
            outs.append(o.reshape(n_s, n_slots * HEAD_DIM))
            lses.append(lse.reshape(n_s, n_slots * HEAD_DIM))
            wks[p].append(nk[0])
            wvs[p].append(nv[0])
        o_at_s = merge_patterns(outs, lses)

        mix_hg = jnp.concatenate([o_hg_p, o_hg_s], axis=0)
        mix_at = jnp.concatenate([o_at_p, o_at_s], axis=0)
        h = matmul([mix_hg, mix_at], w_out[l], res=h)

        nm = rmsnorm(mem_prompt.reshape(b * M, D), norm_mem_g[l])
        mk, mv = matmul([nm], wk_x[l]), matmul([nm], wv_x[l])
        mkp.append(mk.reshape(b, M, MEM_HEADS, D // MEM_HEADS))
        mvp.append(mv.reshape(b, M, MEM_HEADS, D // MEM_HEADS))
        qx = matmul([rmsnorm(h, norm_x_g[l])], wq_x[l])
        a_p = mem_attention_prompt(qx, mk, mv, b, L, M, MEM_HEADS)
        a_s = mem_attention_sample(qx[n_p:].reshape(db, dn, MEM_HEADS, D // MEM_HEADS),
                                   cache_mem_k[l:l + 1], cache_mem_v[l:l + 1])
        att = jnp.concatenate([a_p, a_s.reshape(n_s, D).astype(BF16)], axis=0)
        h = matmul([att], wo_x[l], res=h)

        n3 = rmsnorm(h, norm_ffn_g[l])
        gates = peer_router(matmul([n3], peer_wq[l]), peer_subkeys[l])
        ffn = peer_experts(n3, gates, peer_u[l], peer_v[l])
        if l + 1 < depth:
            h = add_rmsnorm(h, ffn, None)
    y_p, y_s = add_rmsnorm_split(h, ffn, norm_final_g, n_p)
    y_prompt, y_sample = y_p.reshape(b, L, D), y_s.reshape(db, dn, D)
    st = lambda xs: jnp.stack(xs)
    return (y_prompt, y_sample, st(hgp), st(wkp[0]), st(wvp[0]), st(wkp[1]), st(wvp[1]), st(wkp[2]), st(wvp[2]),
            st(mkp), st(mvp), st(hgs), st(wks[0]), st(wvs[0]), st(wks[1]), st(wvs[1]), st(wks[2]), st(wvs[2]))
```

```python
import functools
import math

import jax
import jax.numpy as jnp
from jax import lax
from jax.experimental import pallas as pl
from jax.experimental.pallas import tpu as pltpu

F32 = jnp.float32
BF16 = jnp.bfloat16
NEG_INF = -1e30
RMS_EPS = 1e-6
HEAD_DIM = 128
HG_CHUNK = 16
DIL_PATTERNS = ((128, 1), (512, 4), (2048, 16))
ALIBI_MAX = 8.0
MEM_HEADS = 4
PK_HEADS = 8
N_KEYS = 128
PK_TOPK = 16

VMEM_LIMIT_BYTES = 56 * 1024 * 1024


def _params(*sem):
    return pltpu.CompilerParams(dimension_semantics=sem, vmem_limit_bytes=VMEM_LIMIT_BYTES)


def _row_tile(n, target):
    best = None
    for t in range(8, min(n, target) + 1, 8):
        if n % t == 0:
            best = t
    assert best is not None, (n, target)
    return best


def _rmsnorm_kernel(x_ref, g_ref, o_ref):
    x = x_ref[...]
    ms = jnp.mean(x * x, axis=-1, keepdims=True)
    o_ref[...] = (x * lax.rsqrt(ms + RMS_EPS) * g_ref[...]).astype(o_ref.dtype)


def rmsnorm(x, g, out_dtype=BF16, tm=512):
    T, D = x.shape
    tm = _row_tile(T, tm)
    return pl.pallas_call(
        _rmsnorm_kernel,
        out_shape=jax.ShapeDtypeStruct((T, D), out_dtype),
        grid=(T // tm,),
        in_specs=[pl.BlockSpec((tm, D), lambda i: (i, 0)), pl.BlockSpec((1, D), lambda i: (0, 0))],
        out_specs=pl.BlockSpec((tm, D), lambda i: (i, 0)),
        compiler_params=_params("parallel"),
        name="rmsnorm",
    )(x, g.reshape(1, D).astype(F32))


def _rmsnorm_cat_kernel(xa_ref, xb_ref, g_ref, n_ref, h_ref, *, blocks_a):
    def emit(x):
        h_ref[...] = x
        ms = jnp.mean(x * x, axis=-1, keepdims=True)
        n_ref[...] = (x * lax.rsqrt(ms + RMS_EPS) * g_ref[...]).astype(n_ref.dtype)

    @pl.when(pl.program_id(0) < blocks_a)
    def _():
        emit(xa_ref[...])

    @pl.when(pl.program_id(0) >= blocks_a)
    def _():
        emit(xb_ref[...])


def rmsnorm_cat(xa, xb, g, tm=256):
    (na, D), nb = xa.shape, xb.shape[0]
    tm = _row_tile(math.gcd(na, nb), tm)
    ba, T = na // tm, na + nb
    out = pl.BlockSpec((tm, D), lambda i: (i, 0))
    return pl.pallas_call(
        functools.partial(_rmsnorm_cat_kernel, blocks_a=ba),
        out_shape=(jax.ShapeDtypeStruct((T, D), BF16), jax.ShapeDtypeStruct((T, D), F32)),
        grid=(T // tm,),
        in_specs=[pl.BlockSpec((tm, D), lambda i: (jnp.minimum(i, ba - 1), 0)),
                  pl.BlockSpec((tm, D), lambda i: (jnp.maximum(i - ba, 0), 0)),
                  pl.BlockSpec((1, D), lambda i: (0, 0))],
        out_specs=(out, out),
        compiler_params=_params("arbitrary"),
        name="rmsnorm_cat",
    )(xa, xb, g.reshape(1, D).astype(F32))


def _add_rmsnorm_split_kernel(x_ref, y_ref, g_ref, oa_ref, ob_ref, *, blocks_a):
    x = x_ref[...] + y_ref[...]
    ms = jnp.mean(x * x, axis=-1, keepdims=True)
    x = x * lax.rsqrt(ms + RMS_EPS) * g_ref[...]

    @pl.when(pl.program_id(0) < blocks_a)
    def _():
        oa_ref[...] = x

    @pl.when(pl.program_id(0) >= blocks_a)
    def _():
        ob_ref[...] = x


def add_rmsnorm_split(x, y, g, na, tm=256):
    T, D = x.shape
    tm = _row_tile(math.gcd(na, T - na), tm)
    ba = na // tm
    blk = pl.BlockSpec((tm, D), lambda i: (i, 0))
    return pl.pallas_call(
        functools.partial(_add_rmsnorm_split_kernel, blocks_a=ba),
        out_shape=(jax.ShapeDtypeStruct((na, D), F32), jax.ShapeDtypeStruct((T - na, D), F32)),
        grid=(T // tm,),
        in_specs=[blk, blk, pl.BlockSpec((1, D), lambda i: (0, 0))],
        out_specs=(pl.BlockSpec((tm, D), lambda i: (jnp.minimum(i, ba - 1), 0)),
                   pl.BlockSpec((tm, D), lambda i: (jnp.maximum(i - ba, 0), 0))),
        compiler_params=_params("arbitrary"),
        name="add_rmsnorm_split",
    )(x, y, g.reshape(1, D).astype(F32))


def _add_rmsnorm_kernel(x_ref, y_ref, g_ref, o_ref, *, normalize):
    x = x_ref[...] + y_ref[...]
    if normalize:
        ms = jnp.mean(x * x, axis=-1, keepdims=True)
        x = x * lax.rsqrt(ms + RMS_EPS) * g_ref[...]
    o_ref[...] = x.astype(o_ref.dtype)


def add_rmsnorm(x, y, g, tm=256):
    T, D = x.shape
    tm = _row_tile(T, tm)
    gain = jnp.ones((1, D), F32) if g is None else g.reshape(1, D).astype(F32)
    return pl.pallas_call(
        functools.partial(_add_rmsnorm_kernel, normalize=g is not None),
        out_shape=jax.ShapeDtypeStruct((T, D), F32),
        grid=(T // tm,),
        in_specs=[pl.BlockSpec((tm, D), lambda i: (i, 0)), pl.BlockSpec((tm, D), lambda i: (i, 0)),
                  pl.BlockSpec((1, D), lambda i: (0, 0))],
        out_specs=pl.BlockSpec((tm, D), lambda i: (i, 0)),
        compiler_params=_params("parallel"),
        name="add_rmsnorm",
    )(x, y, gain)


def _mm_kernel(*refs, k_splits, has_res):
    n_a = len(k_splits)
    a_refs, w_ref = refs[:n_a], refs[n_a]
    res_ref = refs[n_a + 1] if has_res else None
    o_ref = refs[-1]
    acc, off = None, 0
    for a_ref, kk in zip(a_refs, k_splits):
        part = jnp.dot(a_ref[...], w_ref[off:off + kk, :].astype(BF16), preferred_element_type=F32)
        acc = part if acc is None else acc + part
        off += kk
    if has_res:
        acc = acc + res_ref[...]
    o_ref[...] = acc.astype(o_ref.dtype)


def matmul(a_list, w, res=None, out_dtype=F32, tm=1088, tn=512):
    M = a_list[0].shape[0]
    k_splits = tuple(a.shape[1] for a in a_list)
    K, N = w.shape
    assert sum(k_splits) == K
    tm, tn = _row_tile(M, tm), min(tn, N)
    assert N % tn == 0
    in_specs = [pl.BlockSpec((tm, kk), lambda i, j: (i, 0)) for kk in k_splits]
    in_specs.append(pl.BlockSpec((K, tn), lambda i, j: (0, j)))
    args = list(a_list) + [w]
    if res is not None:
        in_specs.append(pl.BlockSpec((tm, tn), lambda i, j: (i, j)))
        args.append(res)
    return pl.pallas_call(
        functools.partial(_mm_kernel, k_splits=k_splits, has_res=res is not None),
        out_shape=jax.ShapeDtypeStruct((M, N), out_dtype),
        grid=(M // tm, N // tn),
        in_specs=in_specs,
        out_specs=pl.BlockSpec((tm, tn), lambda i, j: (i, j)),
        compiler_params=_params("parallel", "arbitrary"),
        name="matmul",
    )(*args)


def _silu(x):
    return x * jax.nn.sigmoid(x)


def _split3(x):
    hi = x.astype(BF16)
    r1 = x - hi.astype(F32)
    mid = r1.astype(BF16)
    lo = (r1 - mid.astype(F32)).astype(BF16)
    return hi, mid, lo


def _rows_bcast(x, row_ids, reps):
    W = x.shape[1]
    parts = []
    for r in row_ids:
        row = jnp.zeros((1, W), x.dtype) if r < 0 else x[r:r + 1, :]
        parts.append(jnp.broadcast_to(row, (reps, W)))
    return parts[0] if len(parts) == 1 else jnp.concatenate(parts, axis=0)


def _dot_nt(a, b):
    return lax.dot_general(a, b, (((1,), (1,)), ((), ())), preferred_element_type=F32)


def _dot_tn(a, b):
    return lax.dot_general(a, b, (((0,), (0,)), ((), ())), preferred_element_type=F32)


def _hgrn_chunk(q, k, g, v, st, leaf):
    C = q.shape[0]
    row = lax.broadcasted_iota(jnp.int32, (C, q.shape[1]), 0)
    it = lax.broadcasted_iota(jnp.int32, (C, C), 0)
    js = lax.broadcasted_iota(jnp.int32, (C, C), 1)
    tri = jnp.where(it >= js, 1.0, 0.0).astype(BF16)
    g_hi, g_mid, g_lo = _split3(g)
    cum = (jnp.dot(tri, g_hi, preferred_element_type=F32) + jnp.dot(tri, g_mid, preferred_element_type=F32)
           + jnp.dot(tri, g_lo, preferred_element_type=F32))
    last = cum[C - 1:C, :]

    a = jnp.zeros((C, C), F32)
    bs = C
    while bs > leaf:
        half = bs // 2
        upper = (row % bs) >= half
        cmid = _rows_bcast(cum, [b0 + half - 1 for b0 in range(0, C, bs)], bs)
        e = jnp.exp(jnp.where(upper, cum - cmid, cmid - cum))
        q_l = jnp.where(upper, q * e, 0.0).astype(BF16)
        k_l = jnp.where(upper, 0.0, k * e).astype(BF16)
        a = a + jnp.where((it // bs) == (js // bs), _dot_nt(q_l, k_l), 0.0)
        bs = half
    cstart = _rows_bcast(cum, [b0 - 1 for b0 in range(0, C, leaf)], leaf)
    q_f = (q * jnp.exp(cum - cstart)).astype(BF16)
    k_f = (k * jnp.exp(cstart - cum)).astype(BF16)
    a = a + jnp.where(((it // leaf) == (js // leaf)) & (it >= js), _dot_nt(q_f, k_f), 0.0)

    vb = v.astype(BF16)
    q_g = (q * jnp.exp(cum)).astype(BF16)
    o = jnp.dot(a.astype(BF16), vb, preferred_element_type=F32) + _dot_nt(q_g, st.astype(BF16))
    k_e = (k * jnp.exp(last - cum)).astype(BF16)
    st_new = st * jnp.exp(last) + _dot_tn(vb, k_e)
    return o, st_new


def _hgrn_gates(hq, hf, lb):
    q = _silu(hq)
    f = lb + (1.0 - lb) * jax.nn.sigmoid(hf)
    return q, 1.0 - f, jnp.log(f)


def _hgrn_out(o, hgate, ng):
    ms = jnp.mean(o * o, axis=-1, keepdims=True)
    return o * lax.rsqrt(ms + RMS_EPS) * ng * _silu(hgate)


def _hgrn_prompt_kernel(hq_ref, hf_ref, hi_ref, hg_ref, lb_ref, ng_ref, o_ref, s_ref, st_scr, *, chunk, leaf):
    c = pl.program_id(2)

    @pl.when(c == 0)
    def _():
        st_scr[...] = jnp.zeros_like(st_scr)

    rows, width = hq_ref.shape
    ng = ng_ref[...]
    for s0 in range(0, rows, chunk):
        sl = slice(s0, s0 + chunk)
        for hh in range(width // HEAD_DIM):
            cs = slice(hh * HEAD_DIM, (hh + 1) * HEAD_DIM)
            q, k, g = _hgrn_gates(hq_ref[sl, cs], hf_ref[sl, cs], lb_ref[:, cs])
            o, st_new = _hgrn_chunk(q, k, g, hi_ref[sl, cs], st_scr[hh], leaf)
            st_scr[hh] = st_new
            o_ref[sl, cs] = _hgrn_out(o, hg_ref[sl, cs], ng).astype(o_ref.dtype)

    @pl.when(c == pl.num_programs(2) - 1)
    def _():
        for hh in range(width // HEAD_DIM):
            s_ref[0, hh] = st_scr[hh].T


def hgrn_prompt(proj, lb, ng, b, L, n_heads, col_blocks, rows_per_step=512, chunk=128, leaf=HG_CHUNK,
                heads_per_step=4):
    rows = min(rows_per_step, L)
    hps = heads_per_step
    assert L % rows == 0 and rows % chunk == 0 and n_heads % hps == 0 and all(cb % hps == 0 for cb in col_blocks)
    nc = L // rows
    width = hps * HEAD_DIM

    def in_spec(cb):
        return pl.BlockSpec((rows, width), lambda bi, h, c: (bi * nc + c, cb // hps + h))

    return pl.pallas_call(
        functools.partial(_hgrn_prompt_kernel, chunk=chunk, leaf=leaf),
        out_shape=(jax.ShapeDtypeStruct((b * L, n_heads * HEAD_DIM), BF16),
                   jax.ShapeDtypeStruct((b, n_heads, HEAD_DIM, HEAD_DIM), F32)),
        grid=(b, n_heads // hps, nc),
        in_specs=[in_spec(cb) for cb in col_blocks] + [
            pl.BlockSpec((1, width), lambda bi, h, c: (0, h)),
            pl.BlockSpec((1, HEAD_DIM), lambda bi, h, c: (0, 0))],
        out_specs=(pl.BlockSpec((rows, width), lambda bi, h, c: (bi * nc + c, h)),
                   pl.BlockSpec((1, hps, HEAD_DIM, HEAD_DIM), lambda bi, h, c: (bi, h, 0, 0))),
        scratch_shapes=[pltpu.VMEM((hps, HEAD_DIM, HEAD_DIM), F32)],
        compiler_params=_params("parallel", "parallel", "arbitrary"),
        name="hgrn_prompt",
    )(proj, proj, proj, proj, lb, ng)


def _alibi_slopes(n_pat, n_slots):
    k = jnp.arange(1, n_pat * n_slots + 1, dtype=F32)
    return (2.0 ** (-ALIBI_MAX * k / (n_pat * n_slots))).reshape(n_pat, n_slots)


def _strided_rows(ref, start, size, stride):
    if stride == 1:
        return ref[pl.ds(start, size), :]
    return ref[pl.ds(start, size, stride=stride), :]


def _band_softmax(q, kk, vv, bias, valid, scale):
    s = _dot_nt(q.astype(BF16), kk.astype(BF16)) * scale - bias
    s = jnp.where(valid, s, NEG_INF)
    m = jnp.max(s, axis=-1, keepdims=True)
    p = jnp.exp(s - m)
    l = jnp.sum(p, axis=-1, keepdims=True)
    o = jnp.dot(p.astype(BF16), vv.astype(BF16), preferred_element_type=F32) / l
    return o, m + jnp.log(l)


def _dil_attn_fused_kernel(*refs, patterns):
    n = len(patterns)
    out_ref = refs[6 * n]
    scr = refs[6 * n + 1:]
    for p, (dil, band, units) in enumerate(patterns):
        _dil_attn_kernel(*refs[6 * p:6 * p + 6], scr[2 * p], scr[2 * p + 1], dil=dil, band=band, units=units)
    ls = [scr[2 * p + 1][...] for p in range(n)]
    m = functools.reduce(jnp.maximum, ls)
    ws = [jnp.exp(l - m) for l in ls]
    num = functools.reduce(lambda a, c: a + c, [w * scr[2 * p][...] for p, w in enumerate(ws)])
    den = functools.reduce(lambda a, c: a + c, ws)
    out_ref[...] = (num / den).astype(out_ref.dtype)


def dilated_attention_prompt_fused(proj, slopes, b, L, n_slots, qcol, kcol, vcol, patterns):
    spans = [win for win, dil in patterns]
    rows = max(spans)
    assert L % rows == 0 and all(rows % s == 0 for s in spans)
    nblk = L // rows
    in_specs, args, static = [], [], []
    for p, (win, dil) in enumerate(patterns):
        band = win // dil
        span, units = band * dil, rows // (band * dil)
        static.append((dil, band, units))
        off = p * n_slots
        cur = lambda c0: pl.BlockSpec((rows, HEAD_DIM), lambda bi, h, i, c0=c0: (bi * nblk + i, c0 + h))
        prev = lambda c0, span=span, units=units: pl.BlockSpec(
            (span, HEAD_DIM), lambda bi, h, i: (bi * nblk * units + jnp.maximum(i * units - 1, 0), c0 + h))
        in_specs += [cur(qcol + off), prev(kcol + off), cur(kcol + off), prev(vcol + off), cur(vcol + off),
                     pl.BlockSpec((1, 1, 2 * band), lambda bi, h, i: (h, 0, 0))]
        args += [proj] * 5 + [jnp.broadcast_to(slopes[p].reshape(n_slots, 1, 1), (n_slots, 1, 2 * band)).astype(F32)]
    return pl.pallas_call(
        functools.partial(_dil_attn_fused_kernel, patterns=tuple(static)),
        out_shape=jax.ShapeDtypeStruct((b * L, n_slots * HEAD_DIM), BF16),
        grid=(b, n_slots, nblk),
        in_specs=in_specs,
        out_specs=pl.BlockSpec((rows, HEAD_DIM), lambda bi, h, i: (bi * nblk + i, h)),
        scratch_shapes=[pltpu.VMEM((rows, HEAD_DIM), F32)] * (2 * len(patterns)),
        compiler_params=_params("parallel", "parallel", "arbitrary"),
        name="dil_attn_prompt",
    )(*args)


def _dil_attn_kernel(q_ref, kp_ref, kc_ref, vp_ref, vc_ref, slope_ref, o_ref, lse_ref, *, dil, band, units):
    first = pl.program_id(2) == 0
    span = band * dil
    scale = HEAD_DIM ** -0.5
    qi = lax.broadcasted_iota(jnp.int32, (band, 2 * band), 0)
    kj = lax.broadcasted_iota(jnp.int32, (band, 2 * band), 1)
    off = qi + band - kj
    in_band = (off >= 0) & (off <= band)
    bias = slope_ref[0] * float(dil) * off.astype(F32)
    for u in range(units):
        for r in range(dil):
            base = u * span + r
            q = _strided_rows(q_ref, base, band, dil)
            if u == 0:
                kp, vp = _strided_rows(kp_ref, r, band, dil), _strided_rows(vp_ref, r, band, dil)
                valid = in_band & (jnp.logical_not(first) | (kj >= band))
            else:
                kp, vp = _strided_rows(kc_ref, base - span, band, dil), _strided_rows(vc_ref, base - span, band, dil)
                valid = in_band
            kk = jnp.concatenate([kp, _strided_rows(kc_ref, base, band, dil)], axis=0)
            vv = jnp.concatenate([vp, _strided_rows(vc_ref, base, band, dil)], axis=0)
            o, lse = _band_softmax(q, kk, vv, bias, valid, scale)
            lse_b = jnp.broadcast_to(lse, (band, HEAD_DIM))
            if dil == 1:
                o_ref[pl.ds(base, band), :] = o
                lse_ref[pl.ds(base, band), :] = lse_b
            else:
                o_ref[pl.ds(base, band, stride=dil), :] = o
                lse_ref[pl.ds(base, band, stride=dil), :] = lse_b


def dilated_attention_prompt(proj, slopes_p, b, L, n_slots, qcol, kcol, vcol, win, dil, units):
    band = win // dil
    span = band * dil
    rows = span * units
    assert L % rows == 0
    nblk = L // rows
    cur = lambda c0: pl.BlockSpec((rows, HEAD_DIM), lambda bi, h, i: (bi * nblk + i, c0 + h))
    prev = lambda c0: pl.BlockSpec(
        (span, HEAD_DIM), lambda bi, h, i: (bi * nblk * units + jnp.maximum(i * units - 1, 0), c0 + h))
    out = pl.BlockSpec((rows, HEAD_DIM), lambda bi, h, i: (bi * nblk + i, h))
    shape = jax.ShapeDtypeStruct((b * L, n_slots * HEAD_DIM), F32)
    slope_rep = jnp.broadcast_to(slopes_p.reshape(n_slots, 1, 1), (n_slots, 1, 2 * band)).astype(F32)
    return pl.pallas_call(
        functools.partial(_dil_attn_kernel, dil=dil, band=band, units=units),
        out_shape=(shape, shape),
        grid=(b, n_slots, nblk),
        in_specs=[cur(qcol), prev(kcol), cur(kcol), prev(vcol), cur(vcol),
                  pl.BlockSpec((1, 1, 2 * band), lambda bi, h, i: (h, 0, 0))],
        out_specs=(out, out),
        compiler_params=_params("parallel", "parallel", "arbitrary"),
        name=f"dil_attn_prompt_d{dil}",
    )(proj, proj, proj, proj, proj, slope_rep)


def _merge_kernel(*refs):
    n = (len(refs) - 1) // 2
    o_refs, l_refs, out_ref = refs[:n], refs[n:2 * n], refs[-1]
    ls = [r[...] for r in l_refs]
    m = functools.reduce(jnp.maximum, ls)
    ws = [jnp.exp(l - m) for l in ls]
    num = functools.reduce(lambda a, c: a + c, [w * r[...] for w, r in zip(ws, o_refs)])
    den = functools.reduce(lambda a, c: a + c, ws)
    out_ref[...] = (num / den).astype(out_ref.dtype)


def merge_patterns(outs, lses, tm=512):
    T, W = outs[0].shape
    tm = _row_tile(T, tm)
    spec = pl.BlockSpec((tm, W), lambda i: (i, 0))
    return pl.pallas_call(
        _merge_kernel,
        out_shape=jax.ShapeDtypeStruct((T, W), BF16),
        grid=(T // tm,),
        in_specs=[spec] * (2 * len(outs)),
        out_specs=spec,
        compiler_params=_params("parallel"),
        name="merge_patterns",
    )(*outs, *lses)


def _mem_attn_kernel(q_ref, k_ref, v_ref, o_ref):
    scale = q_ref.shape[-1] ** -0.5
    s = _dot_nt(q_ref[...].astype(BF16), k_ref[...].astype(BF16)) * scale
    m = jnp.max(s, axis=-1, keepdims=True)
    p = jnp.exp(s - m)
    l = jnp.sum(p, axis=-1, keepdims=True)
    o = jnp.dot(p.astype(BF16), v_ref[...].astype(BF16), preferred_element_type=F32) / l
    o_ref[...] = o.astype(o_ref.dtype)


def mem_attention_prompt(q, mk, mv, b, L, M, n_heads, tq=1024):
    D = mk.shape[1]
    hd = D // n_heads
    tq = _row_tile(L, tq)
    nq = L // tq
    return pl.pallas_call(
        _mem_attn_kernel,
        out_shape=jax.ShapeDtypeStruct((b * L, D), BF16),
        grid=(b, n_heads, nq),
        in_specs=[pl.BlockSpec((tq, hd), lambda bi, h, i: (bi * nq + i, h)),
                  pl.BlockSpec((M, hd), lambda bi, h, i: (bi, h)),
                  pl.BlockSpec((M, hd), lambda bi, h, i: (bi, h))],
        out_specs=pl.BlockSpec((tq, hd), lambda bi, h, i: (bi * nq + i, h)),
        compiler_params=_params("parallel", "parallel", "arbitrary"),
        name="mem_attn_prompt",
    )(q, mk, mv)


def _dot_nt_x3(a, b):
    a_hi, b_hi = a.astype(BF16), b.astype(BF16)
    a_lo, b_lo = (a - a_hi.astype(F32)).astype(BF16), (b - b_hi.astype(F32)).astype(BF16)
    return _dot_nt(a_hi, b_hi) + _dot_nt(a_hi, b_lo) + _dot_nt(a_lo, b_hi)


def _topk_rows(x, k, payload=None):
    n = x.shape[0]
    iota = lax.broadcasted_iota(jnp.int32, x.shape, 0).astype(F32)
    vals, outs = [], []
    for _ in range(k):
        m = jnp.max(x, axis=0, keepdims=True)
        pos = jnp.min(jnp.where(x == m, iota, float(n)), axis=0, keepdims=True)
        sel = iota == pos
        vals.append(m)
        outs.append(pos if payload is None else jnp.max(jnp.where(sel, payload, -1.0), axis=0, keepdims=True))
        x = jnp.where(sel, -jnp.inf, x)
    return jnp.concatenate(vals, axis=0), jnp.concatenate(outs, axis=0)


def _candidates(v1, i1, v2, i2, kk):
    vals, idxs, a = [], [], 0
    while a < kk and kk // (a + 1) >= 2:
        nb = min(kk, -(-(kk // (a + 1)) // 8) * 8)
        vals.append(v1[a:a + 1, :] + v2[:nb, :])
        idxs.append(i1[a:a + 1, :] * float(N_KEYS) + i2[:nb, :])
        a += 1
    if a < kk:
        vals.append(v1[a:, :] + v2[0:1, :])
        idxs.append(i1[a:, :] * float(N_KEYS) + i2[0:1, :])
    return jnp.concatenate(vals, axis=0), jnp.concatenate(idxs, axis=0)


_STAGE_PITCH = N_KEYS + 8


def _router_kernel(qp_ref, sk_ref, g_ref, ii_scr, ij_scr, w_scr, iit_scr, ijt_scr, wt_scr, stage_scr):
    tb = qp_ref.shape[0]
    kk = PK_TOPK
    for h in range(PK_HEADS):
        tops = []
        for c in range(2):
            col = (2 * h + c) * HEAD_DIM
            sc = _dot_nt_x3(sk_ref[2 * h + c], qp_ref[:, col:col + HEAD_DIM])
            tops.append(_topk_rows(sc, kk))
        (v1, i1), (v2, i2) = tops
        cand, cidx = _candidates(v1, i1, v2, i2, kk)
        best, eidx = _topk_rows(cand, kk, payload=cidx)
        e = jnp.exp(best - best[0:1, :])
        gate = e / jnp.sum(e, axis=0, keepdims=True)
        ei = jnp.floor(eidx * (1.0 / N_KEYS))
        ii_scr[h * kk:(h + 1) * kk, :] = ei
        ij_scr[h * kk:(h + 1) * kk, :] = eidx - ei * float(N_KEYS)
        w_scr[h * kk:(h + 1) * kk, :] = gate
    iit_scr[...] = ii_scr[...].T
    ijt_scr[...] = ij_scr[...].T
    wt_scr[...] = w_scr[...].T
    sub = lax.broadcasted_iota(jnp.int32, (N_KEYS, PK_HEADS * kk), 0).astype(F32)

    def per_token(t, carry):
        a = jnp.where(sub == iit_scr[pl.ds(t, 1), :], wt_scr[pl.ds(t, 1), :], 0.0).astype(BF16)
        bsel = jnp.where(sub == ijt_scr[pl.ds(t, 1), :], 1.0, 0.0).astype(BF16)
        row0 = pl.multiple_of(t * _STAGE_PITCH, 8)
        stage_scr[pl.ds(row0, N_KEYS), :] = _dot_nt(a, bsel)
        return carry

    lax.fori_loop(0, tb, per_token, 0, unroll=32)
    for i in range(N_KEYS):
        g_ref[:, i * N_KEYS:(i + 1) * N_KEYS] = stage_scr[pl.ds(i, tb, stride=_STAGE_PITCH), :].astype(g_ref.dtype)


def peer_router(qp, subkeys, tb=128):
    T = qp.shape[0]
    assert T % tb == 0 and tb == N_KEYS
    sk = subkeys.reshape(PK_HEADS * 2, N_KEYS, HEAD_DIM)
    slots = PK_HEADS * PK_TOPK
    return pl.pallas_call(
        _router_kernel,
        out_shape=jax.ShapeDtypeStruct((T, N_KEYS * N_KEYS), BF16),
        grid=(T // tb,),
        in_specs=[pl.BlockSpec((tb, qp.shape[1]), lambda i: (i, 0)),
                  pl.BlockSpec(sk.shape, lambda i: (0, 0, 0))],
        out_specs=pl.BlockSpec((tb, N_KEYS * N_KEYS), lambda i: (i, 0)),
        scratch_shapes=[pltpu.VMEM((slots, tb), F32)] * 3 + [pltpu.VMEM((tb, slots), F32)] * 3
        + [pltpu.VMEM((tb * _STAGE_PITCH, N_KEYS), F32)],
        compiler_params=_params("parallel"),
        name="peer_router",
    )(qp, sk)


def _gelu(x):
    return 0.5 * x * (1.0 + lax.erf(x * (2.0 ** -0.5)))


def _cast_kernel(x_ref, o_ref):
    o_ref[...] = x_ref[...].astype(o_ref.dtype)


def cast(x, dtype, tm=512):
    R, C = x.shape
    tm = _row_tile(R, tm)
    spec = pl.BlockSpec((tm, C), lambda i: (i, 0))
    return pl.pallas_call(
        _cast_kernel, out_shape=jax.ShapeDtypeStruct((R, C), dtype), grid=(R // tm,), in_specs=[spec],
        out_specs=spec, compiler_params=_params("parallel"), name="cast",
    )(x)


def _peer_kernel(n_ref, g_ref, u_ref, v_ref, o_ref, hid_a, hid_b):
    j = pl.program_id(1)

    @pl.when(j == 0)
    def _():
        hid_b[...] = jnp.zeros_like(hid_b)
        o_ref[...] = jnp.zeros_like(o_ref)

    def step(prev_scr, next_scr):
        part = jnp.dot(prev_scr[...], v_ref[...].astype(BF16), preferred_element_type=F32)
        xu = _dot_nt(n_ref[...], u_ref[...].astype(BF16))
        next_scr[...] = (g_ref[...].astype(F32) * _gelu(xu)).astype(BF16)
        o_ref[...] += part

    @pl.when(j % 2 == 0)
    def _():
        step(hid_b, hid_a)

    @pl.when(j % 2 == 1)
    def _():
        step(hid_a, hid_b)


def peer_experts(n, gates, u, v, tm=1088, te=256):
    T, D = n.shape
    E = u.shape[0]
    tm = _row_tile(T, tm)
    assert E % te == 0
    ne = E // te
    resident = pl.Buffered(1)
    return pl.pallas_call(
        _peer_kernel,
        out_shape=jax.ShapeDtypeStruct((T, D), F32),
        grid=(T // tm, ne + 1),
        in_specs=[pl.BlockSpec((tm, D), lambda i, j: (i, 0), pipeline_mode=resident),
                  pl.BlockSpec((tm, te), lambda i, j: (i, jnp.minimum(j, ne - 1))),
                  pl.BlockSpec((te, D), lambda i, j: (jnp.minimum(j, ne - 1), 0)),
                  pl.BlockSpec((te, D), lambda i, j: (jnp.maximum(j - 1, 0), 0))],
        out_specs=pl.BlockSpec((tm, D), lambda i, j: (i, 0), pipeline_mode=resident),
        scratch_shapes=[pltpu.VMEM((tm, te), BF16)] * 2,
        compiler_params=_params("parallel", "arbitrary"),
        name="peer_experts",
    )(n, gates, u, v)


def _hgrn_sample_kernel(hq_ref, hf_ref, hi_ref, hg_ref, lb_ref, ng_ref, s_ref, o_ref, so_ref, *, n_tok):
    rows, width = hq_ref.shape
    n_seq, n_head = rows // n_tok, width // HEAD_DIM
    q, k, g = _hgrn_gates(hq_ref[...], hf_ref[...], lb_ref[...])
    t = lax.broadcasted_iota(jnp.int32, (rows, width), 0) % n_tok
    cum, sh = g, 1
    while sh < n_tok:
        cum = cum + jnp.where(t >= sh, pltpu.roll(cum, sh, axis=0), 0.0)
        sh *= 2
    last = _rows_bcast(cum, [s * n_tok + n_tok - 1 for s in range(n_seq)], n_tok)
    q_in = q * jnp.exp(cum)
    k_in = k * jnp.exp(-cum)
    k_out = k * jnp.exp(last - cum)
    dec_t = jnp.exp(last).T
    v = hi_ref[...]
    it = lax.broadcasted_iota(jnp.int32, (n_tok, n_tok), 0)
    js = lax.broadcasted_iota(jnp.int32, (n_tok, n_tok), 1)
    ng = ng_ref[...]
    for s in range(n_seq):
        r = slice(s * n_tok, (s + 1) * n_tok)
        for h in range(n_head):
            c = slice(h * HEAD_DIM, (h + 1) * HEAD_DIM)
            st = s_ref[0, s, h]
            qh, vh = q_in[r, c].astype(BF16), v[r, c].astype(BF16)
            a = jnp.where(it >= js, _dot_nt(qh, k_in[r, c].astype(BF16)), 0.0)
            o = jnp.dot(a.astype(BF16), vh, preferred_element_type=F32)
            o = o + jnp.dot(qh, st.astype(BF16), preferred_element_type=F32)
            dec = dec_t[c, s * n_tok:s * n_tok + 1]
            so_ref[0, s, h] = st * dec + _dot_tn(k_out[r, c].astype(BF16), vh)
            o_ref[r, c] = _hgrn_out(o, hg_ref[r, c], ng).astype(o_ref.dtype)


def hgrn_sample(proj, row0, state, lb, ng, n_tok, col_blocks, seqs_per_step=4, heads_per_step=4):
    _, n_seq, n_heads, _, _ = state.shape
    rows, width = seqs_per_step * n_tok, heads_per_step * HEAD_DIM
    assert n_seq % seqs_per_step == 0 and n_heads % heads_per_step == 0 and row0 % rows == 0
    r0, nhb = row0 // rows, n_heads // heads_per_step

    def in_spec(cb):
        return pl.BlockSpec((rows, width), lambda i, j: (r0 + i, cb // heads_per_step + j))

    st_spec = pl.BlockSpec((1, seqs_per_step, heads_per_step, HEAD_DIM, HEAD_DIM), lambda i, j: (0, i, j, 0, 0))
    return pl.pallas_call(
        functools.partial(_hgrn_sample_kernel, n_tok=n_tok),
        out_shape=(jax.ShapeDtypeStruct((n_seq * n_tok, n_heads * HEAD_DIM), BF16),
                   jax.ShapeDtypeStruct(state.shape, F32)),
        grid=(n_seq // seqs_per_step, nhb),
        in_specs=[in_spec(cb) for cb in col_blocks] + [
            pl.BlockSpec((1, width), lambda i, j: (0, j)),
            pl.BlockSpec((1, HEAD_DIM), lambda i, j: (0, 0)),
            st_spec],
        out_specs=(pl.BlockSpec((rows, width), lambda i, j: (i, j)), st_spec),
        compiler_params=_params("parallel", "parallel"),
        name="hgrn_sample",
    )(proj, proj, proj, proj, lb, ng, state)


def _lane_sum_rep(x):
    ones = jnp.ones((x.shape[1], HEAD_DIM), BF16)
    hi = x.astype(BF16)
    lo = (x - hi.astype(F32)).astype(BF16)
    return jnp.dot(hi, ones, preferred_element_type=F32) + jnp.dot(lo, ones, preferred_element_type=F32)


def _dil_attn_sample_kernel(q_ref, kn_ref, vn_ref, ck_ref, cv_ref, ckx_ref, cvx_ref, slope_ref,
                            o_ref, lse_ref, ok_ref, ov_ref, m_scr, l_scr, acc_scr, *, dil, band):
    n, H, D = q_ref.shape[1:]
    rc = ck_ref.shape[2]
    c, nc = pl.program_id(1), pl.num_programs(1)
    scale = D ** -0.5
    slope = slope_ref[...] * float(dil)

    last = c == nc - 1
    for src, nxt, new, dst in ((ck_ref, ckx_ref, kn_ref, ok_ref), (cv_ref, cvx_ref, vn_ref, ov_ref)):
        dst[0, 0, 0:rc - n] = src[0, 0, n:rc]
        dst[0, 0, rc - n:rc] = jnp.where(last, new[0], nxt[0, 0])

    @pl.when(c == 0)
    def _():
        for i in range(n):
            q = q_ref[0, i]
            news = [ip for ip in range(i + 1) if (i - ip) % dil == 0]
            s_n = [_lane_sum_rep(kn_ref[0, ip] * q) * scale - slope * float((i - ip) // dil) for ip in news]
            m = functools.reduce(jnp.maximum, s_n)
            p_n = [jnp.exp(s - m) for s in s_n]
            m_scr[i] = m
            l_scr[i] = functools.reduce(lambda a, b: a + b, p_n)
            acc_scr[i] = functools.reduce(lambda a, b: a + b, [p * vn_ref[0, ip] for p, ip in zip(p_n, news)])

    rows = rc // dil if dil >= n else band
    mm = lax.broadcasted_iota(jnp.int32, (rows, H, D), 0)
    for i in range(n):
        q = q_ref[0, i]
        if dil >= n:
            kc = ck_ref[0, 0, pl.ds(i, rows, stride=dil)]
            vc = cv_ref[0, 0, pl.ds(i, rows, stride=dil)]
            jc = (nc - c) * rows - mm
            ok_c = None
        else:
            kc = ck_ref[0, 0, pl.ds(rc - band, band)]
            vc = cv_ref[0, 0, pl.ds(rc - band, band)]
            jc = band + i - mm
            ok_c = jc <= band
        s_c = _lane_sum_rep((kc * q[None]).reshape(rows * H, D)).reshape(rows, H, D) * scale
        s_c = s_c - slope[None] * jc.astype(F32)
        if ok_c is not None:
            s_c = jnp.where(ok_c, s_c, NEG_INF)
        m_old = m_scr[i]
        m_new = jnp.maximum(m_old, jnp.max(s_c, axis=0))
        alpha = jnp.exp(m_old - m_new)
        p_c = jnp.exp(s_c - m_new[None])
        l_scr[i] = l_scr[i] * alpha + jnp.sum(p_c, axis=0)
        acc_scr[i] = acc_scr[i] * alpha + jnp.sum(p_c * vc, axis=0)
        m_scr[i] = m_new

    @pl.when(last)
    def _():
        for i in range(n):
            o_ref[0, i] = acc_scr[i] / l_scr[i]
            lse_ref[0, i] = m_scr[i] + jnp.log(l_scr[i])


def dilated_attention_sample(q, k_new, v_new, cache_k, cache_v, slopes_p, win, dil, chunk_rows=512):
    B, n, H, D = q.shape
    keep = cache_k.shape[2]
    band = win // dil
    assert keep == band * dil and keep > n and keep % n == 0, "window cache must hold exactly one window"
    rc = keep if dil < n else min(keep, chunk_rows)
    assert keep % rc == 0 and rc % dil == 0 and rc % n == 0 and (dil >= n or dil == 1)
    nc = keep // rc
    new_spec = pl.BlockSpec((1, n, H, D), lambda b, c: (b, 0, 0, 0))
    c_spec = pl.BlockSpec((1, 1, rc, H, D), lambda b, c: (0, b, c, 0, 0))
    x_spec = pl.BlockSpec((1, 1, n, H, D), lambda b, c: (0, b, jnp.minimum((c + 1) * (rc // n), keep // n - 1), 0, 0))
    shape = jax.ShapeDtypeStruct((B, n, H, D), F32)
    slope_rep = jnp.broadcast_to(slopes_p.reshape(H, 1), (H, D)).astype(F32)
    return pl.pallas_call(
        functools.partial(_dil_attn_sample_kernel, dil=dil, band=band),
        out_shape=(shape, shape, jax.ShapeDtypeStruct(cache_k.shape, cache_k.dtype),
                   jax.ShapeDtypeStruct(cache_v.shape, cache_v.dtype)),
        grid=(B, nc),
        in_specs=[new_spec, new_spec, new_spec, c_spec, c_spec, x_spec, x_spec,
                  pl.BlockSpec((H, D), lambda b, c: (0, 0))],
        out_specs=(new_spec, new_spec, c_spec, c_spec),
        scratch_shapes=[pltpu.VMEM((n, H, D), F32)] * 3,
        compiler_params=_params("parallel", "arbitrary"),
        name=f"dil_attn_sample_d{dil}",
    )(q, k_new, v_new, cache_k, cache_v, cache_k, cache_v, slope_rep)


def _mem_attn_sample_kernel(q_ref, k_ref, v_ref, o_ref):
    n, H, D = q_ref.shape[1:]
    M = k_ref.shape[2]
    scale = D ** -0.5
    k2 = k_ref[0, 0].reshape(M * H, D).astype(BF16)
    v2 = v_ref[0, 0].reshape(M * H, D).astype(BF16)
    q2 = q_ref[0].reshape(n * H, D).astype(BF16)
    s = _dot_nt(q2, k2) * scale
    qh = lax.broadcasted_iota(jnp.int32, s.shape, 0) % H
    kh = lax.broadcasted_iota(jnp.int32, s.shape, 1) % H
    s = jnp.where(qh == kh, s, NEG_INF)
    m = jnp.max(s, axis=-1, keepdims=True)
    p = jnp.exp(s - m)
    l = jnp.sum(p, axis=-1, keepdims=True)
    o = jnp.dot(p.astype(BF16), v2, preferred_element_type=F32) / l
    o_ref[0] = o.reshape(n, H, D)


def mem_attention_sample(q, mem_k, mem_v):
    B, n, H, D = q.shape
    M = mem_k.shape[2]
    q_spec = pl.BlockSpec((1, n, H, D), lambda i: (i, 0, 0, 0))
    m_spec = pl.BlockSpec((1, 1, M, H, D), lambda i: (0, i, 0, 0, 0))
    return pl.pallas_call(
        _mem_attn_sample_kernel,
        out_shape=jax.ShapeDtypeStruct((B, n, H, D), F32),
        grid=(B,),
        in_specs=[q_spec, m_spec, m_spec],
        out_specs=q_spec,
        compiler_params=_params("parallel"),
        name="mem_attn_sample",
    )(q, mem_k, mem_v)


def kernel(x_prompt, x_sample, mem_prompt, state_hgrn, cache_w1_k, cache_w1_v, cache_w2_k, cache_w2_v, cache_w3_k, cache_w3_v, cache_mem_k, cache_mem_v, norm_mix_g, w_in, hg_lower_bound, hg_norm_g, w_out, norm_x_g, norm_mem_g, wq_x, wk_x, wv_x, wo_x, norm_ffn_g, peer_wq, peer_subkeys, peer_u, peer_v, norm_final_g):
    b, L, D = x_prompt.shape
    db, dn, _ = x_sample.shape
    depth = w_in.shape[0]
    n_p, n_s = b * L, db * dn
    hg_heads = state_hgrn.shape[2]
    n_slots = cache_w1_k.shape[3]
    n_pat = len(DIL_PATTERNS)
    M = mem_prompt.shape[1]
    hg_cols = (0, hg_heads, 2 * hg_heads, 3 * hg_heads)
    qcol, kcol, vcol = (4 * hg_heads + s * n_pat * n_slots for s in range(3))
    lbs = jnp.cumsum(jax.nn.softmax(hg_lower_bound.astype(F32), axis=0), axis=0)
    slopes = _alibi_slopes(n_pat, n_slots)
    cache_k = (cache_w1_k, cache_w2_k, cache_w3_k)
    cache_v = (cache_w1_v, cache_w2_v, cache_w3_v)

    n1, h = rmsnorm_cat(x_prompt.reshape(n_p, D), x_sample.reshape(n_s, D), norm_mix_g[0])
    hgp, hgs, mkp, mvp = [], [], [], []
    wkp, wvp, wks, wvs = ([[] for _ in DIL_PATTERNS] for _ in range(4))
    for l in range(depth):
        lb, ng = lbs[l].reshape(1, -1), hg_norm_g[l].reshape(1, -1)
        proj = matmul([n1 if l == 0 else rmsnorm(h, norm_mix_g[l])], w_in[l])

        def head_cols(col0, p, rows):
            c0 = (col0 + p * n_slots) * HEAD_DIM
            return rows[:, c0:c0 + n_slots * HEAD_DIM]

        o_hg_p, s_p = hgrn_prompt(proj, lb, ng, b, L, hg_heads, hg_cols)
        o_at_p = dilated_attention_prompt_fused(proj, slopes, b, L, n_slots, qcol, kcol, vcol, DIL_PATTERNS)
        for p, (win, dil) in enumerate(DIL_PATTERNS):
            keep = min(win, L)
            for col0, dst in ((kcol, wkp), (vcol, wvp)):
                tails = [head_cols(col0, p, proj[(bi + 1) * L - keep:(bi + 1) * L]) for bi in range(b)]
                dst[p].append(jnp.stack(tails).reshape(b, keep, n_slots, HEAD_DIM))
        hgp.append(s_p)

        o_hg_s, s_s = hgrn_sample(proj, n_p, state_hgrn[l:l + 1], lb, ng, dn, hg_cols)
        hgs.append(s_s[0])
        proj_s = proj[n_p:]
        outs, lses = [], []
        for p, (win, dil) in enumerate(DIL_PATTERNS):
            q_s, k_s, v_s = (head_cols(c, p, proj_s).reshape(db, dn, n_slots, HEAD_DIM) for c in (qcol, kcol, vcol))
            ck, cv = cache_k[p][l:l + 1], cache_v[p][l:l + 1]
            o, lse, nk, nv = dilated_attention_sample(q_s, k_s, v_s, ck, cv, slopes[p], win, dil)
            outs.append(o.reshape(n_s, n_slots * HEAD_DIM))
            lses.append(lse.reshape(n_s, n_slots * HEAD_DIM))
            wks[p].append(nk[0])
            wvs[p].append(nv[0])
        o_at_s = merge_patterns(outs, lses)

        mix_hg = jnp.concatenate([o_hg_p, o_hg_s], axis=0)
        mix_at = jnp.concatenate([o_at_p, o_at_s], axis=0)
        h = matmul([mix_hg, mix_at], w_out[l], res=h)

        nm = rmsnorm(mem_prompt.reshape(b * M, D), norm_mem_g[l])
        mk, mv = matmul([nm], wk_x[l]), matmul([nm], wv_x[l])
        mkp.append(mk.reshape(b, M, MEM_HEADS, D // MEM_HEADS))
        mvp.append(mv.reshape(b, M, MEM_HEADS, D // MEM_HEADS))
        qx = matmul([rmsnorm(h, norm_x_g[l])], wq_x[l])
        a_p = mem_attention_prompt(qx, mk, mv, b, L, M, MEM_HEADS)
        a_s = mem_attention_sample(qx[n_p:].reshape(db, dn, MEM_HEADS, D // MEM_HEADS),
                                   cache_mem_k[l:l + 1], cache_mem_v[l:l + 1])
        att = jnp.concatenate([a_p, a_s.reshape(n_s, D).astype(BF16)], axis=0)
        h = matmul([att], wo_x[l], res=h)

        n3 = rmsnorm(h, norm_ffn_g[l])
        gates = peer_router(matmul([n3], peer_wq[l]), peer_subkeys[l])
        ffn = peer_experts(n3, gates, peer_u[l], peer_v[l])
        if l + 1 < depth:
            h = add_rmsnorm(h, ffn, None)
    y_p, y_s = add_rmsnorm_split(h, ffn, norm_final_g, n_p)
    y_prompt, y_sample = y_p.reshape(b, L, D), y_s.reshape(db, dn, D)
    st = lambda xs: jnp.stack(xs)
    return (y_prompt, y_sample, st(hgp), st(wkp[0]), st(wvp[0]), st(wkp[1]), st(wvp[1]), st(wkp[2]), st(wvp[2]),
            st(mkp), st(mvp), st(hgs), st(wks[0]), st(wvs[0]), st(wks[1]), st(wvs[1]), st(wks[2]), st(wvs[2]))
```

```python
import functools
import math

import jax
import jax.numpy as jnp
from jax import lax
from jax.experimental import pallas as pl
from jax.experimental.pallas import tpu as pltpu

F32 = jnp.float32
BF16 = jnp.bfloat16
NEG_INF = -1e30
RMS_EPS = 1e-6
HEAD_DIM = 128
HG_CHUNK = 16
DIL_PATTERNS = ((128, 1), (512, 4), (2048, 16))
ALIBI_MAX = 8.0
MEM_HEADS = 4
PK_HEADS = 8
N_KEYS = 128
PK_TOPK = 16

VMEM_LIMIT_BYTES = 56 * 1024 * 1024


def _params(*sem):
    return pltpu.CompilerParams(dimension_semantics=sem, vmem_limit_bytes=VMEM_LIMIT_BYTES)


def _row_tile(n, target):
    best = None
    for t in range(8, min(n, target) + 1, 8):
        if n % t == 0:
            best = t
    assert best is not None, (n, target)
    return best


def _rmsnorm_kernel(x_ref, g_ref, o_ref):
    x = x_ref[...]
    ms = jnp.mean(x * x, axis=-1, keepdims=True)
    o_ref[...] = (x * lax.rsqrt(ms + RMS_EPS) * g_ref[...]).astype(o_ref.dtype)


def rmsnorm(x, g, out_dtype=BF16, tm=512):
    T, D = x.shape
    tm = _row_tile(T, tm)
    return pl.pallas_call(
        _rmsnorm_kernel,
        out_shape=jax.ShapeDtypeStruct((T, D), out_dtype),
        grid=(T // tm,),
        in_specs=[pl.BlockSpec((tm, D), lambda i: (i, 0)), pl.BlockSpec((1, D), lambda i: (0, 0))],
        out_specs=pl.BlockSpec((tm, D), lambda i: (i, 0)),
        compiler_params=_params("parallel"),
        name="rmsnorm",
    )(x, g.reshape(1, D).astype(F32))


def _rmsnorm_cat_kernel(xa_ref, xb_ref, g_ref, n_ref, h_ref, *, blocks_a):
    def emit(x):
        h_ref[...] = x
        ms = jnp.mean(x * x, axis=-1, keepdims=True)
        n_ref[...] = (x * lax.rsqrt(ms + RMS_EPS) * g_ref[...]).astype(n_ref.dtype)

    @pl.when(pl.program_id(0) < blocks_a)
    def _():
        emit(xa_ref[...])

    @pl.when(pl.program_id(0) >= blocks_a)
    def _():
        emit(xb_ref[...])


def rmsnorm_cat(xa, xb, g, tm=256):
    (na, D), nb = xa.shape, xb.shape[0]
    tm = _row_tile(math.gcd(na, nb), tm)
    ba, T = na // tm, na + nb
    out = pl.BlockSpec((tm, D), lambda i: (i, 0))
    return pl.pallas_call(
        functools.partial(_rmsnorm_cat_kernel, blocks_a=ba),
        out_shape=(jax.ShapeDtypeStruct((T, D), BF16), jax.ShapeDtypeStruct((T, D), F32)),
        grid=(T // tm,),
        in_specs=[pl.BlockSpec((tm, D), lambda i: (jnp.minimum(i, ba - 1), 0)),
                  pl.BlockSpec((tm, D), lambda i: (jnp.maximum(i - ba, 0), 0)),
                  pl.BlockSpec((1, D), lambda i: (0, 0))],
        out_specs=(out, out),
        compiler_params=_params("arbitrary"),
        name="rmsnorm_cat",
    )(xa, xb, g.reshape(1, D).astype(F32))


def _add_rmsnorm_split_kernel(x_ref, y_ref, g_ref, oa_ref, ob_ref, *, blocks_a):
    x = x_ref[...] + y_ref[...]
    ms = jnp.mean(x * x, axis=-1, keepdims=True)
    x = x * lax.rsqrt(ms + RMS_EPS) * g_ref[...]

    @pl.when(pl.program_id(0) < blocks_a)
    def _():
        oa_ref[...] = x

    @pl.when(pl.program_id(0) >= blocks_a)
    def _():
        ob_ref[...] = x


def add_rmsnorm_split(x, y, g, na, tm=256):
    T, D = x.shape
    tm = _row_tile(math.gcd(na, T - na), tm)
    ba = na // tm
    blk = pl.BlockSpec((tm, D), lambda i: (i, 0))
    return pl.pallas_call(
        functools.partial(_add_rmsnorm_split_kernel, blocks_a=ba),
        out_shape=(jax.ShapeDtypeStruct((na, D), F32), jax.ShapeDtypeStruct((T - na, D), F32)),
        grid=(T // tm,),
        in_specs=[blk, blk, pl.BlockSpec((1, D), lambda i: (0, 0))],
        out_specs=(pl.BlockSpec((tm, D), lambda i: (jnp.minimum(i, ba - 1), 0)),
                   pl.BlockSpec((tm, D), lambda i: (jnp.maximum(i - ba, 0), 0))),
        compiler_params=_params("arbitrary"),
        name="add_rmsnorm_split",
    )(x, y, g.reshape(1, D).astype(F32))


def _add_rmsnorm_kernel(x_ref, y_ref, g_ref, o_ref, *, normalize):
    x = x_ref[...] + y_ref[...]
    if normalize:
        ms = jnp.mean(x * x, axis=-1, keepdims=True)
        x = x * lax.rsqrt(ms + RMS_EPS) * g_ref[...]
    o_ref[...] = x.astype(o_ref.dtype)


def add_rmsnorm(x, y, g, tm=256):
    T, D = x.shape
    tm = _row_tile(T, tm)
    gain = jnp.ones((1, D), F32) if g is None else g.reshape(1, D).astype(F32)
    return pl.pallas_call(
        functools.partial(_add_rmsnorm_kernel, normalize=g is not None),
        out_shape=jax.ShapeDtypeStruct((T, D), F32),
        grid=(T // tm,),
        in_specs=[pl.BlockSpec((tm, D), lambda i: (i, 0)), pl.BlockSpec((tm, D), lambda i: (i, 0)),
                  pl.BlockSpec((1, D), lambda i: (0, 0))],
        out_specs=pl.BlockSpec((tm, D), lambda i: (i, 0)),
        compiler_params=_params("parallel"),
        name="add_rmsnorm",
    )(x, y, gain)


def _mm_kernel(*refs, k_splits, has_res):
    n_a = len(k_splits)
    a_refs, w_ref = refs[:n_a], refs[n_a]
    res_ref = refs[n_a + 1] if has_res else None
    o_ref = refs[-1]
    acc, off = None, 0
    for a_ref, kk in zip(a_refs, k_splits):
        part = jnp.dot(a_ref[...], w_ref[off:off + kk, :].astype(BF16), preferred_element_type=F32)
        acc = part if acc is None else acc + part
        off += kk
    if has_res:
        acc = acc + res_ref[...]
    o_ref[...] = acc.astype(o_ref.dtype)


def matmul(a_list, w, res=None, out_dtype=F32, tm=1088, tn=512):
    M = a_list[0].shape[0]
    k_splits = tuple(a.shape[1] for a in a_list)
    K, N = w.shape
    assert sum(k_splits) == K
    tm, tn = _row_tile(M, tm), min(tn, N)
    assert N % tn == 0
    in_specs = [pl.BlockSpec((tm, kk), lambda i, j: (i, 0)) for kk in k_splits]
    in_specs.append(pl.BlockSpec((K, tn), lambda i, j: (0, j)))
    args = list(a_list) + [w]
    if res is not None:
        in_specs.append(pl.BlockSpec((tm, tn), lambda i, j: (i, j)))
        args.append(res)
    return pl.pallas_call(
        functools.partial(_mm_kernel, k_splits=k_splits, has_res=res is not None),
        out_shape=jax.ShapeDtypeStruct((M, N), out_dtype),
        grid=(M // tm, N // tn),
        in_specs=in_specs,
        out_specs=pl.BlockSpec((tm, tn), lambda i, j: (i, j)),
        compiler_params=_params("parallel", "arbitrary"),
        name="matmul",
    )(*args)


def _silu(x):
    return x * jax.nn.sigmoid(x)


def _split3(x):
    hi = x.astype(BF16)
    r1 = x - hi.astype(F32)
    mid = r1.astype(BF16)
    lo = (r1 - mid.astype(F32)).astype(BF16)
    return hi, mid, lo


def _rows_bcast(x, row_ids, reps):
    W = x.shape[1]
    parts = []
    for r in row_ids:
        row = jnp.zeros((1, W), x.dtype) if r < 0 else x[r:r + 1, :]
        parts.append(jnp.broadcast_to(row, (reps, W)))
    return parts[0] if len(parts) == 1 else jnp.concatenate(parts, axis=0)


def _dot_nt(a, b):
    return lax.dot_general(a, b, (((1,), (1,)), ((), ())), preferred_element_type=F32)


def _dot_tn(a, b):
    return lax.dot_general(a, b, (((0,), (0,)), ((), ())), preferred_element_type=F32)


def _hgrn_chunk(q, k, g, v, st, leaf):
    C = q.shape[0]
    row = lax.broadcasted_iota(jnp.int32, (C, q.shape[1]), 0)
    it = lax.broadcasted_iota(jnp.int32, (C, C), 0)
    js = lax.broadcasted_iota(jnp.int32, (C, C), 1)
    tri = jnp.where(it >= js, 1.0, 0.0).astype(BF16)
    g_hi, g_mid, g_lo = _split3(g)
    cum = (jnp.dot(tri, g_hi, preferred_element_type=F32) + jnp.dot(tri, g_mid, preferred_element_type=F32)
           + jnp.dot(tri, g_lo, preferred_element_type=F32))
    last = cum[C - 1:C, :]

    a = jnp.zeros((C, C), F32)
    bs = C
    while bs > leaf:
        half = bs // 2
        upper = (row % bs) >= half
        cmid = _rows_bcast(cum, [b0 + half - 1 for b0 in range(0, C, bs)], bs)
        e = jnp.exp(jnp.where(upper, cum - cmid, cmid - cum))
        q_l = jnp.where(upper, q * e, 0.0).astype(BF16)
        k_l = jnp.where(upper, 0.0, k * e).astype(BF16)
        a = a + jnp.where((it // bs) == (js // bs), _dot_nt(q_l, k_l), 0.0)
        bs = half
    cstart = _rows_bcast(cum, [b0 - 1 for b0 in range(0, C, leaf)], leaf)
    q_f = (q * jnp.exp(cum - cstart)).astype(BF16)
    k_f = (k * jnp.exp(cstart - cum)).astype(BF16)
    a = a + jnp.where(((it // leaf) == (js // leaf)) & (it >= js), _dot_nt(q_f, k_f), 0.0)

    vb = v.astype(BF16)
    q_g = (q * jnp.exp(cum)).astype(BF16)
    o = jnp.dot(a.astype(BF16), vb, preferred_element_type=F32) + _dot_nt(q_g, st.astype(BF16))
    k_e = (k * jnp.exp(last - cum)).astype(BF16)
    st_new = st * jnp.exp(last) + _dot_tn(vb, k_e)
    return o, st_new


def _hgrn_gates(hq, hf, lb):
    q = _silu(hq)
    f = lb + (1.0 - lb) * jax.nn.sigmoid(hf)
    return q, 1.0 - f, jnp.log(f)


def _hgrn_out(o, hgate, ng):
    ms = jnp.mean(o * o, axis=-1, keepdims=True)
    return o * lax.rsqrt(ms + RMS_EPS) * ng * _silu(hgate)


def _hgrn_prompt_kernel(hq_ref, hf_ref, hi_ref, hg_ref, lb_ref, ng_ref, o_ref, s_ref, st_scr, *, chunk, leaf):
    c = pl.program_id(2)

    @pl.when(c == 0)
    def _():
        st_scr[...] = jnp.zeros_like(st_scr)

    rows, width = hq_ref.shape
    ng = ng_ref[...]
    for s0 in range(0, rows, chunk):
        sl = slice(s0, s0 + chunk)
        for hh in range(width // HEAD_DIM):
            cs = slice(hh * HEAD_DIM, (hh + 1) * HEAD_DIM)
            q, k, g = _hgrn_gates(hq_ref[sl, cs], hf_ref[sl, cs], lb_ref[:, cs])
            o, st_new = _hgrn_chunk(q, k, g, hi_ref[sl, cs], st_scr[hh], leaf)
            st_scr[hh] = st_new
            o_ref[sl, cs] = _hgrn_out(o, hg_ref[sl, cs], ng).astype(o_ref.dtype)

    @pl.when(c == pl.num_programs(2) - 1)
    def _():
        for hh in range(width // HEAD_DIM):
            s_ref[0, hh] = st_scr[hh].T


def hgrn_prompt(proj, lb, ng, b, L, n_heads, col_blocks, rows_per_step=512, chunk=128, leaf=HG_CHUNK,
                heads_per_step=4):
    rows = min(rows_per_step, L)
    hps = heads_per_step
    assert L % rows == 0 and rows % chunk == 0 and n_heads % hps == 0 and all(cb % hps == 0 for cb in col_blocks)
    nc = L // rows
    width = hps * HEAD_DIM

    def in_spec(cb):
        return pl.BlockSpec((rows, width), lambda bi, h, c: (bi * nc + c, cb // hps + h))

    return pl.pallas_call(
        functools.partial(_hgrn_prompt_kernel, chunk=chunk, leaf=leaf),
        out_shape=(jax.ShapeDtypeStruct((b * L, n_heads * HEAD_DIM), BF16),
                   jax.ShapeDtypeStruct((b, n_heads, HEAD_DIM, HEAD_DIM), F32)),
        grid=(b, n_heads // hps, nc),
        in_specs=[in_spec(cb) for cb in col_blocks] + [
            pl.BlockSpec((1, width), lambda bi, h, c: (0, h)),
            pl.BlockSpec((1, HEAD_DIM), lambda bi, h, c: (0, 0))],
        out_specs=(pl.BlockSpec((rows, width), lambda bi, h, c: (bi * nc + c, h)),
                   pl.BlockSpec((1, hps, HEAD_DIM, HEAD_DIM), lambda bi, h, c: (bi, h, 0, 0))),
        scratch_shapes=[pltpu.VMEM((hps, HEAD_DIM, HEAD_DIM), F32)],
        compiler_params=_params("parallel", "parallel", "arbitrary"),
        name="hgrn_prompt",
    )(proj, proj, proj, proj, lb, ng)


def _alibi_slopes(n_pat, n_slots):
    k = jnp.arange(1, n_pat * n_slots + 1, dtype=F32)
    return (2.0 ** (-ALIBI_MAX * k / (n_pat * n_slots))).reshape(n_pat, n_slots)


def _strided_rows(ref, start, size, stride):
    if stride == 1:
        return ref[pl.ds(start, size), :]
    return ref[pl.ds(start, size, stride=stride), :]


def _band_softmax(q, kk, vv, bias, valid, scale):
    s = _dot_nt(q.astype(BF16), kk.astype(BF16)) * scale - bias
    s = jnp.where(valid, s, NEG_INF)
    m = jnp.max(s, axis=-1, keepdims=True)
    p = jnp.exp(s - m)
    l = jnp.sum(p, axis=-1, keepdims=True)
    o = jnp.dot(p.astype(BF16), vv.astype(BF16), preferred_element_type=F32) / l
    return o, m + jnp.log(l)


def _dil_attn_fused_kernel(*refs, patterns):
    n = len(patterns)
    out_ref = refs[6 * n]
    scr = refs[6 * n + 1:]
    for p, (dil, band, units) in enumerate(patterns):
        _dil_attn_kernel(*refs[6 * p:6 * p + 6], scr[2 * p], scr[2 * p + 1], dil=dil, band=band, units=units)
    ls = [scr[2 * p + 1][...] for p in range(n)]
    m = functools.reduce(jnp.maximum, ls)
    ws = [jnp.exp(l - m) for l in ls]
    num = functools.reduce(lambda a, c: a + c, [w * scr[2 * p][...] for p, w in enumerate(ws)])
    den = functools.reduce(lambda a, c: a + c, ws)
    out_ref[...] = (num / den).astype(out_ref.dtype)


def dilated_attention_prompt_fused(proj, slopes, b, L, n_slots, qcol, kcol, vcol, patterns):
    spans = [win for win, dil in patterns]
    rows = max(spans)
    assert L % rows == 0 and all(rows % s == 0 for s in spans)
    nblk = L // rows
    in_specs, args, static = [], [], []
    for p, (win, dil) in enumerate(patterns):
        band = win // dil
        span, units = band * dil, rows // (band * dil)
        static.append((dil, band, units))
        off = p * n_slots
        cur = lambda c0: pl.BlockSpec((rows, HEAD_DIM), lambda bi, h, i, c0=c0: (bi * nblk + i, c0 + h))
        prev = lambda c0, span=span, units=units: pl.BlockSpec(
            (span, HEAD_DIM), lambda bi, h, i: (bi * nblk * units + jnp.maximum(i * units - 1, 0), c0 + h))
        in_specs += [cur(qcol + off), prev(kcol + off), cur(kcol + off), prev(vcol + off), cur(vcol + off),
                     pl.BlockSpec((1, 1, 2 * band), lambda bi, h, i: (h, 0, 0))]
        args += [proj] * 5 + [jnp.broadcast_to(slopes[p].reshape(n_slots, 1, 1), (n_slots, 1, 2 * band)).astype(F32)]
    return pl.pallas_call(
        functools.partial(_dil_attn_fused_kernel, patterns=tuple(static)),
        out_shape=jax.ShapeDtypeStruct((b * L, n_slots * HEAD_DIM), BF16),
        grid=(b, n_slots, nblk),
        in_specs=in_specs,
        out_specs=pl.BlockSpec((rows, HEAD_DIM), lambda bi, h, i: (bi * nblk + i, h)),
        scratch_shapes=[pltpu.VMEM((rows, HEAD_DIM), F32)] * (2 * len(patterns)),
        compiler_params=_params("parallel", "parallel", "arbitrary"),
        name="dil_attn_prompt",
    )(*args)


def _dil_attn_kernel(q_ref, kp_ref, kc_ref, vp_ref, vc_ref, slope_ref, o_ref, lse_ref, *, dil, band, units):
    first = pl.program_id(2) == 0
    span = band * dil
    scale = HEAD_DIM ** -0.5
    qi = lax.broadcasted_iota(jnp.int32, (band, 2 * band), 0)
    kj = lax.broadcasted_iota(jnp.int32, (band, 2 * band), 1)
    off = qi + band - kj
    in_band = (off >= 0) & (off <= band)
    bias = slope_ref[0] * float(dil) * off.astype(F32)
    for u in range(units):
        for r in range(dil):
            base = u * span + r
            q = _strided_rows(q_ref, base, band, dil)
            if u == 0:
                kp, vp = _strided_rows(kp_ref, r, band, dil), _strided_rows(vp_ref, r, band, dil)
                valid = in_band & (jnp.logical_not(first) | (kj >= band))
            else:
                kp, vp = _strided_rows(kc_ref, base - span, band, dil), _strided_rows(vc_ref, base - span, band, dil)
                valid = in_band
            kk = jnp.concatenate([kp, _strided_rows(kc_ref, base, band, dil)], axis=0)
            vv = jnp.concatenate([vp, _strided_rows(vc_ref, base, band, dil)], axis=0)
            o, lse = _band_softmax(q, kk, vv, bias, valid, scale)
            lse_b = jnp.broadcast_to(lse, (band, HEAD_DIM))
            if dil == 1:
                o_ref[pl.ds(base, band), :] = o
                lse_ref[pl.ds(base, band), :] = lse_b
            else:
                o_ref[pl.ds(base, band, stride=dil), :] = o
                lse_ref[pl.ds(base, band, stride=dil), :] = lse_b


def dilated_attention_prompt(proj, slopes_p, b, L, n_slots, qcol, kcol, vcol, win, dil, units):
    band = win // dil
    span = band * dil
    rows = span * units
    assert L % rows == 0
    nblk = L // rows
    cur = lambda c0: pl.BlockSpec((rows, HEAD_DIM), lambda bi, h, i: (bi * nblk + i, c0 + h))
    prev = lambda c0: pl.BlockSpec(
        (span, HEAD_DIM), lambda bi, h, i: (bi * nblk * units + jnp.maximum(i * units - 1, 0), c0 + h))
    out = pl.BlockSpec((rows, HEAD_DIM), lambda bi, h, i: (bi * nblk + i, h))
    shape = jax.ShapeDtypeStruct((b * L, n_slots * HEAD_DIM), F32)
    slope_rep = jnp.broadcast_to(slopes_p.reshape(n_slots, 1, 1), (n_slots, 1, 2 * band)).astype(F32)
    return pl.pallas_call(
        functools.partial(_dil_attn_kernel, dil=dil, band=band, units=units),
        out_shape=(shape, shape),
        grid=(b, n_slots, nblk),
        in_specs=[cur(qcol), prev(kcol), cur(kcol), prev(vcol), cur(vcol),
                  pl.BlockSpec((1, 1, 2 * band), lambda bi, h, i: (h, 0, 0))],
        out_specs=(out, out),
        compiler_params=_params("parallel", "parallel", "arbitrary"),
        name=f"dil_attn_prompt_d{dil}",
    )(proj, proj, proj, proj, proj, slope_rep)


def _merge_kernel(*refs):
    n = (len(refs) - 1) // 2
    o_refs, l_refs, out_ref = refs[:n], refs[n:2 * n], refs[-1]
    ls = [r[...] for r in l_refs]
    m = functools.reduce(jnp.maximum, ls)
    ws = [jnp.exp(l - m) for l in ls]
    num = functools.reduce(lambda a, c: a + c, [w * r[...] for w, r in zip(ws, o_refs)])
    den = functools.reduce(lambda a, c: a + c, ws)
    out_ref[...] = (num / den).astype(out_ref.dtype)


def merge_patterns(outs, lses, tm=512):
    T, W = outs[0].shape
    tm = _row_tile(T, tm)
    spec = pl.BlockSpec((tm, W), lambda i: (i, 0))
    return pl.pallas_call(
        _merge_kernel,
        out_shape=jax.ShapeDtypeStruct((T, W), BF16),
        grid=(T // tm,),
        in_specs=[spec] * (2 * len(outs)),
        out_specs=spec,
        compiler_params=_params("parallel"),
        name="merge_patterns",
    )(*outs, *lses)


def _mem_attn_kernel(q_ref, k_ref, v_ref, o_ref):
    scale = q_ref.shape[-1] ** -0.5
    s = _dot_nt(q_ref[...].astype(BF16), k_ref[...].astype(BF16)) * scale
    m = jnp.max(s, axis=-1, keepdims=True)
    p = jnp.exp(s - m)
    l = jnp.sum(p, axis=-1, keepdims=True)
    o = jnp.dot(p.astype(BF16), v_ref[...].astype(BF16), preferred_element_type=F32) / l
    o_ref[...] = o.astype(o_ref.dtype)


def mem_attention_prompt(q, mk, mv, b, L, M, n_heads, tq=1024):
    D = mk.shape[1]
    hd = D // n_heads
    tq = _row_tile(L, tq)
    nq = L // tq
    return pl.pallas_call(
        _mem_attn_kernel,
        out_shape=jax.ShapeDtypeStruct((b * L, D), BF16),
        grid=(b, n_heads, nq),
        in_specs=[pl.BlockSpec((tq, hd), lambda bi, h, i: (bi * nq + i, h)),
                  pl.BlockSpec((M, hd), lambda bi, h, i: (bi, h)),
                  pl.BlockSpec((M, hd), lambda bi, h, i: (bi, h))],
        out_specs=pl.BlockSpec((tq, hd), lambda bi, h, i: (bi * nq + i, h)),
        compiler_params=_params("parallel", "parallel", "arbitrary"),
        name="mem_attn_prompt",
    )(q, mk, mv)


def _dot_nt_x3(a, b):
    a_hi, b_hi = a.astype(BF16), b.astype(BF16)
    a_lo, b_lo = (a - a_hi.astype(F32)).astype(BF16), (b - b_hi.astype(F32)).astype(BF16)
    return _dot_nt(a_hi, b_hi) + _dot_nt(a_hi, b_lo) + _dot_nt(a_lo, b_hi)


def _topk_rows(x, k, payload=None):
    n = x.shape[0]
    iota = lax.broadcasted_iota(jnp.int32, x.shape, 0).astype(F32)
    vals, outs = [], []
    for _ in range(k):
        m = jnp.max(x, axis=0, keepdims=True)
        pos = jnp.min(jnp.where(x == m, iota, float(n)), axis=0, keepdims=True)
        sel = iota == pos
        vals.append(m)
        outs.append(pos if payload is None else jnp.max(jnp.where(sel, payload, -1.0), axis=0, keepdims=True))
        x = jnp.where(sel, -jnp.inf, x)
    return jnp.concatenate(vals, axis=0), jnp.concatenate(outs, axis=0)


def _candidates(v1, i1, v2, i2, kk):
    vals, idxs, a = [], [], 0
    while a < kk and kk // (a + 1) >= 2:
        nb = min(kk, -(-(kk // (a + 1)) // 8) * 8)
        vals.append(v1[a:a + 1, :] + v2[:nb, :])
        idxs.append(i1[a:a + 1, :] * float(N_KEYS) + i2[:nb, :])
        a += 1
    if a < kk:
        vals.append(v1[a:, :] + v2[0:1, :])
        idxs.append(i1[a:, :] * float(N_KEYS) + i2[0:1, :])
    return jnp.concatenate(vals, axis=0), jnp.concatenate(idxs, axis=0)


_STAGE_PITCH = N_KEYS + 8


def _router_kernel(qp_ref, sk_ref, g_ref, ii_scr, ij_scr, w_scr, iit_scr, ijt_scr, wt_scr, stage_scr):
    tb = qp_ref.shape[0]
    kk = PK_TOPK
    for h in range(PK_HEADS):
        tops = []
        for c in range(2):
            col = (2 * h + c) * HEAD_DIM
            sc = _dot_nt_x3(sk_ref[2 * h + c], qp_ref[:, col:col + HEAD_DIM])
            tops.append(_topk_rows(sc, kk))
        (v1, i1), (v2, i2) = tops
        cand, cidx = _candidates(v1, i1, v2, i2, kk)
        best, eidx = _topk_rows(cand, kk, payload=cidx)
        e = jnp.exp(best - best[0:1, :])
        gate = e / jnp.sum(e, axis=0, keepdims=True)
        ei = jnp.floor(eidx * (1.0 / N_KEYS))
        ii_scr[h * kk:(h + 1) * kk, :] = ei
        ij_scr[h * kk:(h + 1) * kk, :] = eidx - ei * float(N_KEYS)
        w_scr[h * kk:(h + 1) * kk, :] = gate
    iit_scr[...] = ii_scr[...].T
    ijt_scr[...] = ij_scr[...].T
    wt_scr[...] = w_scr[...].T
    sub = lax.broadcasted_iota(jnp.int32, (N_KEYS, PK_HEADS * kk), 0).astype(F32)

    def per_token(t, carry):
        a = jnp.where(sub == iit_scr[pl.ds(t, 1), :], wt_scr[pl.ds(t, 1), :], 0.0).astype(BF16)
        bsel = jnp.where(sub == ijt_scr[pl.ds(t, 1), :], 1.0, 0.0).astype(BF16)
        row0 = pl.multiple_of(t * _STAGE_PITCH, 8)
        stage_scr[pl.ds(row0, N_KEYS), :] = _dot_nt(a, bsel)
        return carry

    lax.fori_loop(0, tb, per_token, 0, unroll=32)
    for i in range(N_KEYS):
        g_ref[:, i * N_KEYS:(i + 1) * N_KEYS] = stage_scr[pl.ds(i, tb, stride=_STAGE_PITCH), :].astype(g_ref.dtype)


def peer_router(qp, subkeys, tb=128):
    T = qp.shape[0]
    assert T % tb == 0 and tb == N_KEYS
    sk = subkeys.reshape(PK_HEADS * 2, N_KEYS, HEAD_DIM)
    slots = PK_HEADS * PK_TOPK
    return pl.pallas_call(
        _router_kernel,
        out_shape=jax.ShapeDtypeStruct((T, N_KEYS * N_KEYS), BF16),
        grid=(T // tb,),
        in_specs=[pl.BlockSpec((tb, qp.shape[1]), lambda i: (i, 0)),
                  pl.BlockSpec(sk.shape, lambda i: (0, 0, 0))],
        out_specs=pl.BlockSpec((tb, N_KEYS * N_KEYS), lambda i: (i, 0)),
        scratch_shapes=[pltpu.VMEM((slots, tb), F32)] * 3 + [pltpu.VMEM((tb, slots), F32)] * 3
        + [pltpu.VMEM((tb * _STAGE_PITCH, N_KEYS), F32)],
        compiler_params=_params("parallel"),
        name="peer_router",
    )(qp, sk)


def _gelu(x):
    return 0.5 * x * (1.0 + lax.erf(x * (2.0 ** -0.5)))


def _cast_kernel(x_ref, o_ref):
    o_ref[...] = x_ref[...].astype(o_ref.dtype)


def cast(x, dtype, tm=512):
    R, C = x.shape
    tm = _row_tile(R, tm)
    spec = pl.BlockSpec((tm, C), lambda i: (i, 0))
    return pl.pallas_call(
        _cast_kernel, out_shape=jax.ShapeDtypeStruct((R, C), dtype), grid=(R // tm,), in_specs=[spec],
        out_specs=spec, compiler_params=_params("parallel"), name="cast",
    )(x)


def _peer_kernel(n_ref, g_ref, u_ref, v_ref, o_ref, hid_a, hid_b):
    j = pl.program_id(1)

    @pl.when(j == 0)
    def _():
        hid_b[...] = jnp.zeros_like(hid_b)
        o_ref[...] = jnp.zeros_like(o_ref)

    def step(prev_scr, next_scr):
        part = jnp.dot(prev_scr[...], v_ref[...].astype(BF16), preferred_element_type=F32)
        xu = _dot_nt(n_ref[...], u_ref[...].astype(BF16))
        next_scr[...] = (g_ref[...].astype(F32) * _gelu(xu)).astype(BF16)
        o_ref[...] += part

    @pl.when(j % 2 == 0)
    def _():
        step(hid_b, hid_a)

    @pl.when(j % 2 == 1)
    def _():
        step(hid_a, hid_b)


def peer_experts(n, gates, u, v, tm=1088, te=256):
    T, D = n.shape
    E = u.shape[0]
    tm = _row_tile(T, tm)
    assert E % te == 0
    ne = E // te
    resident = pl.Buffered(1)
    return pl.pallas_call(
        _peer_kernel,
        out_shape=jax.ShapeDtypeStruct((T, D), F32),
        grid=(T // tm, ne + 1),
        in_specs=[pl.BlockSpec((tm, D), lambda i, j: (i, 0), pipeline_mode=resident),
                  pl.BlockSpec((tm, te), lambda i, j: (i, jnp.minimum(j, ne - 1))),
                  pl.BlockSpec((te, D), lambda i, j: (jnp.minimum(j, ne - 1), 0)),
                  pl.BlockSpec((te, D), lambda i, j: (jnp.maximum(j - 1, 0), 0))],
        out_specs=pl.BlockSpec((tm, D), lambda i, j: (i, 0), pipeline_mode=resident),
        scratch_shapes=[pltpu.VMEM((tm, te), BF16)] * 2,
        compiler_params=_params("parallel", "arbitrary"),
        name="peer_experts",
    )(n, gates, u, v)


def _hgrn_sample_kernel(hq_ref, hf_ref, hi_ref, hg_ref, lb_ref, ng_ref, s_ref, o_ref, so_ref, *, n_tok):
    rows, width = hq_ref.shape
    n_seq, n_head = rows // n_tok, width // HEAD_DIM
    q, k, g = _hgrn_gates(hq_ref[...], hf_ref[...], lb_ref[...])
    t = lax.broadcasted_iota(jnp.int32, (rows, width), 0) % n_tok
    cum, sh = g, 1
    while sh < n_tok:
        cum = cum + jnp.where(t >= sh, pltpu.roll(cum, sh, axis=0), 0.0)
        sh *= 2
    last = _rows_bcast(cum, [s * n_tok + n_tok - 1 for s in range(n_seq)], n_tok)
    q_in = q * jnp.exp(cum)
    k_in = k * jnp.exp(-cum)
    k_out = k * jnp.exp(last - cum)
    dec_t = jnp.exp(last).T
    v = hi_ref[...]
    it = lax.broadcasted_iota(jnp.int32, (n_tok, n_tok), 0)
    js = lax.broadcasted_iota(jnp.int32, (n_tok, n_tok), 1)
    ng = ng_ref[...]
    for s in range(n_seq):
        r = slice(s * n_tok, (s + 1) * n_tok)
        for h in range(n_head):
            c = slice(h * HEAD_DIM, (h + 1) * HEAD_DIM)
            st = s_ref[0, s, h]
            qh, vh = q_in[r, c].astype(BF16), v[r, c].astype(BF16)
            a = jnp.where(it >= js, _dot_nt(qh, k_in[r, c].astype(BF16)), 0.0)
            o = jnp.dot(a.astype(BF16), vh, preferred_element_type=F32)
            o = o + jnp.dot(qh, st.astype(BF16), preferred_element_type=F32)
            dec = dec_t[c, s * n_tok:s * n_tok + 1]
            so_ref[0, s, h] = st * dec + _dot_tn(k_out[r, c].astype(BF16), vh)
            o_ref[r, c] = _hgrn_out(o, hg_ref[r, c], ng).astype(o_ref.dtype)


def hgrn_sample(proj, row0, state, lb, ng, n_tok, col_blocks, seqs_per_step=4, heads_per_step=4):
    _, n_seq, n_heads, _, _ = state.shape
    rows, width = seqs_per_step * n_tok, heads_per_step * HEAD_DIM
    assert n_seq % seqs_per_step == 0 and n_heads % heads_per_step == 0 and row0 % rows == 0
    r0, nhb = row0 // rows, n_heads // heads_per_step

    def in_spec(cb):
        return pl.BlockSpec((rows, width), lambda i, j: (r0 + i, cb // heads_per_step + j))

    st_spec = pl.BlockSpec((1, seqs_per_step, heads_per_step, HEAD_DIM, HEAD_DIM), lambda i, j: (0, i, j, 0, 0))
    return pl.pallas_call(
        functools.partial(_hgrn_sample_kernel, n_tok=n_tok),
        out_shape=(jax.ShapeDtypeStruct((n_seq * n_tok, n_heads * HEAD_DIM), BF16),
                   jax.ShapeDtypeStruct(state.shape, F32)),
        grid=(n_seq // seqs_per_step, nhb),
        in_specs=[in_spec(cb) for cb in col_blocks] + [
            pl.BlockSpec((1, width), lambda i, j: (0, j)),
            pl.BlockSpec((1, HEAD_DIM), lambda i, j: (0, 0)),
            st_spec],
        out_specs=(pl.BlockSpec((rows, width), lambda i, j: (i, j)), st_spec),
        compiler_params=_params("parallel", "parallel"),
        name="hgrn_sample",
    )(proj, proj, proj, proj, lb, ng, state)


def _lane_sum_rep(x):
    ones = jnp.ones((x.shape[1], HEAD_DIM), BF16)
    hi = x.astype(BF16)
    lo = (x - hi.astype(F32)).astype(BF16)
    return jnp.dot(hi, ones, preferred_element_type=F32) + jnp.dot(lo, ones, preferred_element_type=F32)


def _dil_attn_sample_kernel(q_ref, kn_ref, vn_ref, ck_ref, cv_ref, ckx_ref, cvx_ref, slope_ref,
                            o_ref, lse_ref, ok_ref, ov_ref, m_scr, l_scr, acc_scr, *, dil, band):
    n, H, D = q_ref.shape[1:]
    rc = ck_ref.shape[2]
    c, nc = pl.program_id(1), pl.num_programs(1)
    scale = D ** -0.5
    slope = slope_ref[...] * float(dil)

    last = c == nc - 1
    for src, nxt, new, dst in ((ck_ref, ckx_ref, kn_ref, ok_ref), (cv_ref, cvx_ref, vn_ref, ov_ref)):
        dst[0, 0, 0:rc - n] = src[0, 0, n:rc]
        dst[0, 0, rc - n:rc] = jnp.where(last, new[0], nxt[0, 0])

    @pl.when(c == 0)
    def _():
        for i in range(n):
            q = q_ref[0, i]
            news = [ip for ip in range(i + 1) if (i - ip) % dil == 0]
            s_n = [_lane_sum_rep(kn_ref[0, ip] * q) * scale - slope * float((i - ip) // dil) for ip in news]
            m = functools.reduce(jnp.maximum, s_n)
            p_n = [jnp.exp(s - m) for s in s_n]
            m_scr[i] = m
            l_scr[i] = functools.reduce(lambda a, b: a + b, p_n)
            acc_scr[i] = functools.reduce(lambda a, b: a + b, [p * vn_ref[0, ip] for p, ip in zip(p_n, news)])

    rows = rc // dil if dil >= n else band
    mm = lax.broadcasted_iota(jnp.int32, (rows, H, D), 0)
    for i in range(n):
        q = q_ref[0, i]
        if dil >= n:
            kc = ck_ref[0, 0, pl.ds(i, rows, stride=dil)]
            vc = cv_ref[0, 0, pl.ds(i, rows, stride=dil)]
            jc = (nc - c) * rows - mm
            ok_c = None
        else:
            kc = ck_ref[0, 0, pl.ds(rc - band, band)]
            vc = cv_ref[0, 0, pl.ds(rc - band, band)]
            jc = band + i - mm
            ok_c = jc <= band
        s_c = _lane_sum_rep((kc * q[None]).reshape(rows * H, D)).reshape(rows, H, D) * scale
        s_c = s_c - slope[None] * jc.astype(F32)
        if ok_c is not None:
            s_c = jnp.where(ok_c, s_c, NEG_INF)
        m_old = m_scr[i]
        m_new = jnp.maximum(m_old, jnp.max(s_c, axis=0))
        alpha = jnp.exp(m_old - m_new)
        p_c = jnp.exp(s_c - m_new[None])
        l_scr[i] = l_scr[i] * alpha + jnp.sum(p_c, axis=0)
        acc_scr[i] = acc_scr[i] * alpha + jnp.sum(p_c * vc, axis=0)
        m_scr[i] = m_new

    @pl.when(last)
    def _():
        for i in range(n):
            o_ref[0, i] = acc_scr[i] / l_scr[i]
            lse_ref[0, i] = m_scr[i] + jnp.log(l_scr[i])


def dilated_attention_sample(q, k_new, v_new, cache_k, cache_v, slopes_p, win, dil, chunk_rows=1024):
    B, n, H, D = q.shape
    keep = cache_k.shape[2]
    band = win // dil
    assert keep == band * dil and keep > n and keep % n == 0, "window cache must hold exactly one window"
    rc = keep if dil < n else min(keep, chunk_rows)
    assert keep % rc == 0 and rc % dil == 0 and rc % n == 0 and (dil >= n or dil == 1)
    nc = keep // rc
    new_spec = pl.BlockSpec((1, n, H, D), lambda b, c: (b, 0, 0, 0))
    c_spec = pl.BlockSpec((1, 1, rc, H, D), lambda b, c: (0, b, c, 0, 0))
    x_spec = pl.BlockSpec((1, 1, n, H, D), lambda b, c: (0, b, jnp.minimum((c + 1) * (rc // n), keep // n - 1), 0, 0))
    shape = jax.ShapeDtypeStruct((B, n, H, D), F32)
    slope_rep = jnp.broadcast_to(slopes_p.reshape(H, 1), (H, D)).astype(F32)
    return pl.pallas_call(
        functools.partial(_dil_attn_sample_kernel, dil=dil, band=band),
        out_shape=(shape, shape, jax.ShapeDtypeStruct(cache_k.shape, cache_k.dtype),
                   jax.ShapeDtypeStruct(cache_v.shape, cache_v.dtype)),
        grid=(B, nc),
        in_specs=[new_spec, new_spec, new_spec, c_spec, c_spec, x_spec, x_spec,
                  pl.BlockSpec((H, D), lambda b, c: (0, 0))],
        out_specs=(new_spec, new_spec, c_spec, c_spec),
        scratch_shapes=[pltpu.VMEM((n, H, D), F32)] * 3,
        compiler_params=_params("parallel", "arbitrary"),
        name=f"dil_attn_sample_d{dil}",
    )(q, k_new, v_new, cache_k, cache_v, cache_k, cache_v, slope_rep)


def _mem_attn_sample_kernel(q_ref, k_ref, v_ref, o_ref):
    n, H, D = q_ref.shape[1:]
    M = k_ref.shape[2]
    scale = D ** -0.5
    k2 = k_ref[0, 0].reshape(M * H, D).astype(BF16)
    v2 = v_ref[0, 0].reshape(M * H, D).astype(BF16)
    q2 = q_ref[0].reshape(n * H, D).astype(BF16)
    s = _dot_nt(q2, k2) * scale
    qh = lax.broadcasted_iota(jnp.int32, s.shape, 0) % H
    kh = lax.broadcasted_iota(jnp.int32, s.shape, 1) % H
    s = jnp.where(qh == kh, s, NEG_INF)
    m = jnp.max(s, axis=-1, keepdims=True)
    p = jnp.exp(s - m)
    l = jnp.sum(p, axis=-1, keepdims=True)
    o = jnp.dot(p.astype(BF16), v2, preferred_element_type=F32) / l
    o_ref[0] = o.reshape(n, H, D)


def mem_attention_sample(q, mem_k, mem_v):
    B, n, H, D = q.shape
    M = mem_k.shape[2]
    q_spec = pl.BlockSpec((1, n, H, D), lambda i: (i, 0, 0, 0))
    m_spec = pl.BlockSpec((1, 1, M, H, D), lambda i: (0, i, 0, 0, 0))
    return pl.pallas_call(
        _mem_attn_sample_kernel,
        out_shape=jax.ShapeDtypeStruct((B, n, H, D), F32),
        grid=(B,),
        in_specs=[q_spec, m_spec, m_spec],
        out_specs=q_spec,
        compiler_params=_params("parallel"),
        name="mem_attn_sample",
    )(q, mem_k, mem_v)


def kernel(x_prompt, x_sample, mem_prompt, state_hgrn, cache_w1_k, cache_w1_v, cache_w2_k, cache_w2_v, cache_w3_k, cache_w3_v, cache_mem_k, cache_mem_v, norm_mix_g, w_in, hg_lower_bound, hg_norm_g, w_out, norm_x_g, norm_mem_g, wq_x, wk_x, wv_x, wo_x, norm_ffn_g, peer_wq, peer_subkeys, peer_u, peer_v, norm_final_g):
    b, L, D = x_prompt.shape
    db, dn, _ = x_sample.shape
    depth = w_in.shape[0]
    n_p, n_s = b * L, db * dn
    hg_heads = state_hgrn.shape[2]
    n_slots = cache_w1_k.shape[3]
    n_pat = len(DIL_PATTERNS)
    M = mem_prompt.shape[1]
    hg_cols = (0, hg_heads, 2 * hg_heads, 3 * hg_heads)
    qcol, kcol, vcol = (4 * hg_heads + s * n_pat * n_slots for s in range(3))
    lbs = jnp.cumsum(jax.nn.softmax(hg_lower_bound.astype(F32), axis=0), axis=0)
    slopes = _alibi_slopes(n_pat, n_slots)
    cache_k = (cache_w1_k, cache_w2_k, cache_w3_k)
    cache_v = (cache_w1_v, cache_w2_v, cache_w3_v)

    n1, h = rmsnorm_cat(x_prompt.reshape(n_p, D), x_sample.reshape(n_s, D), norm_mix_g[0])
    hgp, hgs, mkp, mvp = [], [], [], []
    wkp, wvp, wks, wvs = ([[] for _ in DIL_PATTERNS] for _ in range(4))
    for l in range(depth):
        lb, ng = lbs[l].reshape(1, -1), hg_norm_g[l].reshape(1, -1)
        proj = matmul([n1 if l == 0 else rmsnorm(h, norm_mix_g[l])], w_in[l])

        def head_cols(col0, p, rows):
            c0 = (col0 + p * n_slots) * HEAD_DIM
            return rows[:, c0:c0 + n_slots * HEAD_DIM]

        o_hg_p, s_p = hgrn_prompt(proj, lb, ng, b, L, hg_heads, hg_cols)
        o_at_p = dilated_attention_prompt_fused(proj, slopes, b, L, n_slots, qcol, kcol, vcol, DIL_PATTERNS)
        for p, (win, dil) in enumerate(DIL_PATTERNS):
            keep = min(win, L)
            for col0, dst in ((kcol, wkp), (vcol, wvp)):
                tails = [head_cols(col0, p, proj[(bi + 1) * L - keep:(bi + 1) * L]) for bi in range(b)]
                dst[p].append(jnp.stack(tails).reshape(b, keep, n_slots, HEAD_DIM))
        hgp.append(s_p)

        o_hg_s, s_s = hgrn_sample(proj, n_p, state_hgrn[l:l + 1], lb, ng, dn, hg_cols)
        hgs.append(s_s[0])
        proj_s = proj[n_p:]
        outs, lses = [], []
        for p, (win, dil) in enumerate(DIL_PATTERNS):
            q_s, k_s, v_s = (head_cols(c, p, proj_s).reshape(db, dn, n_slots, HEAD_DIM) for c in (qcol, kcol, vcol))
            ck, cv = cache_k[p][l:l + 1], cache_v[p][l:l + 1]
            o, lse, nk, nv = dilated_attention_sample(q_s, k_s, v_s, ck, cv, slopes[p], win, dil)
            outs.append(o.reshape(n_s, n_slots * HEAD_DIM))
            lses.append(lse.reshape(n_s, n_slots * HEAD_DIM))
            wks[p].append(nk[0])
            wvs[p].append(nv[0])
        o_at_s = merge_patterns(outs, lses)

        mix_hg = jnp.concatenate([o_hg_p, o_hg_s], axis=0)
        mix_at = jnp.concatenate([o_at_p, o_at_s], axis=0)
        h = matmul([mix_hg, mix_at], w_out[l], res=h)

        nm = rmsnorm(mem_prompt.reshape(b * M, D), norm_mem_g[l])
        mk, mv = matmul([nm], wk_x[l]), matmul([nm], wv_x[l])
        mkp.append(mk.reshape(b, M, MEM_HEADS, D // MEM_HEADS))
        mvp.append(mv.reshape(b, M, MEM_HEADS, D // MEM_HEADS))
        qx = matmul([rmsnorm(h, norm_x_g[l])], wq_x[l])
        a_p = mem_attention_prompt(qx, mk, mv, b, L, M, MEM_HEADS)
        a_s = mem_attention_sample(qx[n_p:].reshape(db, dn, MEM_HEADS, D // MEM_HEADS),
                                   cache_mem_k[l:l + 1], cache_mem_v[l:l + 1])
        att = jnp.concatenate([a_p, a_s.reshape(n_s, D).astype(BF16)], axis=0)
        h = matmul([att], wo_x[l], res=h)

        n3 = rmsnorm(h, norm_ffn_g[l])
        gates = peer_router(matmul([n3], peer_wq[l]), peer_subkeys[l])
        ffn = peer_experts(n3, gates, peer_u[l], peer_v[l])
        if l + 1 < depth:
            h = add_rmsnorm(h, ffn, None)
    y_p, y_s = add_rmsnorm_split(h, ffn, norm_final_g, n_p)
    y_prompt, y_sample = y_p.reshape(b, L, D), y_s.reshape(db, dn, D)
    st = lambda xs: jnp.stack(xs)
    return (y_prompt, y_sample, st(hgp), st(wkp[0]), st(wvp[0]), st(wkp[1]), st(wvp[1]), st(wkp[2]), st(wvp[2]),
            st(mkp), st(mvp), st(hgs), st(wks[0]), st(wvs[0]), st(wks[1]), st(wvs[1]), st(wks[2]), st(wvs[2]))
```

```python
import functools
import math

import jax
import jax.numpy as jnp
from jax import lax
from jax.experimental import pallas as pl
from jax.experimental.pallas import tpu as pltpu

F32 = jnp.float32
BF16 = jnp.bfloat16
NEG_INF = -1e30
RMS_EPS = 1e-6
HEAD_DIM = 128
HG_CHUNK = 16
DIL_PATTERNS = ((128, 1), (512, 4), (2048, 16))
ALIBI_MAX = 8.0
MEM_HEADS = 4
PK_HEADS = 8
N_KEYS = 128
PK_TOPK = 16

VMEM_LIMIT_BYTES = 56 * 1024 * 1024


def _params(*sem):
    return pltpu.CompilerParams(dimension_semantics=sem, vmem_limit_bytes=VMEM_LIMIT_BYTES)


def _row_tile(n, target):
    best = None
    for t in range(8, min(n, target) + 1, 8):
        if n % t == 0:
            best = t
    assert best is not None, (n, target)
    return best


def _rmsnorm_kernel(x_ref, g_ref, o_ref):
    x = x_ref[...]
    ms = jnp.mean(x * x, axis=-1, keepdims=True)
    o_ref[...] = (x * lax.rsqrt(ms + RMS_EPS) * g_ref[...]).astype(o_ref.dtype)


def rmsnorm(x, g, out_dtype=BF16, tm=512):
    T, D = x.shape
    tm = _row_tile(T, tm)
    return pl.pallas_call(
        _rmsnorm_kernel,
        out_shape=jax.ShapeDtypeStruct((T, D), out_dtype),
        grid=(T // tm,),
        in_specs=[pl.BlockSpec((tm, D), lambda i: (i, 0)), pl.BlockSpec((1, D), lambda i: (0, 0))],
        out_specs=pl.BlockSpec((tm, D), lambda i: (i, 0)),
        compiler_params=_params("parallel"),
        name="rmsnorm",
    )(x, g.reshape(1, D).astype(F32))


def _rmsnorm_cat_kernel(xa_ref, xb_ref, g_ref, n_ref, h_ref, *, blocks_a):
    def emit(x):
        h_ref[...] = x
        ms = jnp.mean(x * x, axis=-1, keepdims=True)
        n_ref[...] = (x * lax.rsqrt(ms + RMS_EPS) * g_ref[...]).astype(n_ref.dtype)

    @pl.when(pl.program_id(0) < blocks_a)
    def _():
        emit(xa_ref[...])

    @pl.when(pl.program_id(0) >= blocks_a)
    def _():
        emit(xb_ref[...])


def rmsnorm_cat(xa, xb, g, tm=256):
    (na, D), nb = xa.shape, xb.shape[0]
    tm = _row_tile(math.gcd(na, nb), tm)
    ba, T = na // tm, na + nb
    out = pl.BlockSpec((tm, D), lambda i: (i, 0))
    return pl.pallas_call(
        functools.partial(_rmsnorm_cat_kernel, blocks_a=ba),
        out_shape=(jax.ShapeDtypeStruct((T, D), BF16), jax.ShapeDtypeStruct((T, D), F32)),
        grid=(T // tm,),
        in_specs=[pl.BlockSpec((tm, D), lambda i: (jnp.minimum(i, ba - 1), 0)),
                  pl.BlockSpec((tm, D), lambda i: (jnp.maximum(i - ba, 0), 0)),
                  pl.BlockSpec((1, D), lambda i: (0, 0))],
        out_specs=(out, out),
        compiler_params=_params("arbitrary"),
        name="rmsnorm_cat",
    )(xa, xb, g.reshape(1, D).astype(F32))


def _add_rmsnorm_split_kernel(x_ref, y_ref, g_ref, oa_ref, ob_ref, *, blocks_a):
    x = x_ref[...] + y_ref[...]
    ms = jnp.mean(x * x, axis=-1, keepdims=True)
    x = x * lax.rsqrt(ms + RMS_EPS) * g_ref[...]

    @pl.when(pl.program_id(0) < blocks_a)
    def _():
        oa_ref[...] = x

    @pl.when(pl.program_id(0) >= blocks_a)
    def _():
        ob_ref[...] = x


def add_rmsnorm_split(x, y, g, na, tm=256):
    T, D = x.shape
    tm = _row_tile(math.gcd(na, T - na), tm)
    ba = na // tm
    blk = pl.BlockSpec((tm, D), lambda i: (i, 0))
    return pl.pallas_call(
        functools.partial(_add_rmsnorm_split_kernel, blocks_a=ba),
        out_shape=(jax.ShapeDtypeStruct((na, D), F32), jax.ShapeDtypeStruct((T - na, D), F32)),
        grid=(T // tm,),
        in_specs=[blk, blk, pl.BlockSpec((1, D), lambda i: (0, 0))],
        out_specs=(pl.BlockSpec((tm, D), lambda i: (jnp.minimum(i, ba - 1), 0)),
                   pl.BlockSpec((tm, D), lambda i: (jnp.maximum(i - ba, 0), 0))),
        compiler_params=_params("arbitrary"),
        name="add_rmsnorm_split",
    )(x, y, g.reshape(1, D).astype(F32))


def _add_rmsnorm_kernel(x_ref, y_ref, g_ref, o_ref, *, normalize):
    x = x_ref[...] + y_ref[...]
    if normalize:
        ms = jnp.mean(x * x, axis=-1, keepdims=True)
        x = x * lax.rsqrt(ms + RMS_EPS) * g_ref[...]
    o_ref[...] = x.astype(o_ref.dtype)


def add_rmsnorm(x, y, g, tm=256):
    T, D = x.shape
    tm = _row_tile(T, tm)
    gain = jnp.ones((1, D), F32) if g is None else g.reshape(1, D).astype(F32)
    return pl.pallas_call(
        functools.partial(_add_rmsnorm_kernel, normalize=g is not None),
        out_shape=jax.ShapeDtypeStruct((T, D), F32),
        grid=(T // tm,),
        in_specs=[pl.BlockSpec((tm, D), lambda i: (i, 0)), pl.BlockSpec((tm, D), lambda i: (i, 0)),
                  pl.BlockSpec((1, D), lambda i: (0, 0))],
        out_specs=pl.BlockSpec((tm, D), lambda i: (i, 0)),
        compiler_params=_params("parallel"),
        name="add_rmsnorm",
    )(x, y, gain)


def _mm_kernel(*refs, k_splits, has_res):
    n_a = len(k_splits)
    a_refs, w_ref = refs[:n_a], refs[n_a]
    res_ref = refs[n_a + 1] if has_res else None
    o_ref = refs[-1]
    acc, off = None, 0
    for a_ref, kk in zip(a_refs, k_splits):
        part = jnp.dot(a_ref[...], w_ref[off:off + kk, :].astype(BF16), preferred_element_type=F32)
        acc = part if acc is None else acc + part
        off += kk
    if has_res:
        acc = acc + res_ref[...]
    o_ref[...] = acc.astype(o_ref.dtype)


def matmul(a_list, w, res=None, out_dtype=F32, tm=1088, tn=512):
    M = a_list[0].shape[0]
    k_splits = tuple(a.shape[1] for a in a_list)
    K, N = w.shape
    assert sum(k_splits) == K
    tm, tn = _row_tile(M, tm), min(tn, N)
    assert N % tn == 0
    in_specs = [pl.BlockSpec((tm, kk), lambda i, j: (i, 0)) for kk in k_splits]
    in_specs.append(pl.BlockSpec((K, tn), lambda i, j: (0, j)))
    args = list(a_list) + [w]
    if res is not None:
        in_specs.append(pl.BlockSpec((tm, tn), lambda i, j: (i, j)))
        args.append(res)
    return pl.pallas_call(
        functools.partial(_mm_kernel, k_splits=k_splits, has_res=res is not None),
        out_shape=jax.ShapeDtypeStruct((M, N), out_dtype),
        grid=(M // tm, N // tn),
        in_specs=in_specs,
        out_specs=pl.BlockSpec((tm, tn), lambda i, j: (i, j)),
        compiler_params=_params("parallel", "arbitrary"),
        name="matmul",
    )(*args)


def _silu(x):
    return x * jax.nn.sigmoid(x)


def _split3(x):
    hi = x.astype(BF16)
    r1 = x - hi.astype(F32)
    mid = r1.astype(BF16)
    lo = (r1 - mid.astype(F32)).astype(BF16)
    return hi, mid, lo


def _rows_bcast(x, row_ids, reps):
    W = x.shape[1]
    parts = []
    for r in row_ids:
        row = jnp.zeros((1, W), x.dtype) if r < 0 else x[r:r + 1, :]
        parts.append(jnp.broadcast_to(row, (reps, W)))
    return parts[0] if len(parts) == 1 else jnp.concatenate(parts, axis=0)


def _dot_nt(a, b):
    return lax.dot_general(a, b, (((1,), (1,)), ((), ())), preferred_element_type=F32)


def _dot_tn(a, b):
    return lax.dot_general(a, b, (((0,), (0,)), ((), ())), preferred_element_type=F32)


def _hgrn_chunk(q, k, g, v, st, leaf):
    C = q.shape[0]
    row = lax.broadcasted_iota(jnp.int32, (C, q.shape[1]), 0)
    it = lax.broadcasted_iota(jnp.int32, (C, C), 0)
    js = lax.broadcasted_iota(jnp.int32, (C, C), 1)
    tri = jnp.where(it >= js, 1.0, 0.0).astype(BF16)
    g_hi, g_mid, g_lo = _split3(g)
    cum = (jnp.dot(tri, g_hi, preferred_element_type=F32) + jnp.dot(tri, g_mid, preferred_element_type=F32)
           + jnp.dot(tri, g_lo, preferred_element_type=F32))
    last = cum[C - 1:C, :]

    a = jnp.zeros((C, C), F32)
    bs = C
    while bs > leaf:
        half = bs // 2
        upper = (row % bs) >= half
        cmid = _rows_bcast(cum, [b0 + half - 1 for b0 in range(0, C, bs)], bs)
        e = jnp.exp(jnp.where(upper, cum - cmid, cmid - cum))
        q_l = jnp.where(upper, q * e, 0.0).astype(BF16)
        k_l = jnp.where(upper, 0.0, k * e).astype(BF16)
        a = a + jnp.where((it // bs) == (js // bs), _dot_nt(q_l, k_l), 0.0)
        bs = half
    cstart = _rows_bcast(cum, [b0 - 1 for b0 in range(0, C, leaf)], leaf)
    q_f = (q * jnp.exp(cum - cstart)).astype(BF16)
    k_f = (k * jnp.exp(cstart - cum)).astype(BF16)
    a = a + jnp.where(((it // leaf) == (js // leaf)) & (it >= js), _dot_nt(q_f, k_f), 0.0)

    vb = v.astype(BF16)
    q_g = (q * jnp.exp(cum)).astype(BF16)
    o = jnp.dot(a.astype(BF16), vb, preferred_element_type=F32) + _dot_nt(q_g, st.astype(BF16))
    k_e = (k * jnp.exp(last - cum)).astype(BF16)
    st_new = st * jnp.exp(last) + _dot_tn(vb, k_e)
    return o, st_new


def _hgrn_gates(hq, hf, lb):
    q = _silu(hq)
    f = lb + (1.0 - lb) * jax.nn.sigmoid(hf)
    return q, 1.0 - f, jnp.log(f)


def _hgrn_out(o, hgate, ng):
    ms = jnp.mean(o * o, axis=-1, keepdims=True)
    return o * lax.rsqrt(ms + RMS_EPS) * ng * _silu(hgate)


def _hgrn_prompt_kernel(hq_ref, hf_ref, hi_ref, hg_ref, lb_ref, ng_ref, o_ref, s_ref, st_scr, *, chunk, leaf):
    c = pl.program_id(2)

    @pl.when(c == 0)
    def _():
        st_scr[...] = jnp.zeros_like(st_scr)

    rows, width = hq_ref.shape
    ng = ng_ref[...]
    for s0 in range(0, rows, chunk):
        sl = slice(s0, s0 + chunk)
        for hh in range(width // HEAD_DIM):
            cs = slice(hh * HEAD_DIM, (hh + 1) * HEAD_DIM)
            q, k, g = _hgrn_gates(hq_ref[sl, cs], hf_ref[sl, cs], lb_ref[:, cs])
            o, st_new = _hgrn_chunk(q, k, g, hi_ref[sl, cs], st_scr[hh], leaf)
            st_scr[hh] = st_new
            o_ref[sl, cs] = _hgrn_out(o, hg_ref[sl, cs], ng).astype(o_ref.dtype)

    @pl.when(c == pl.num_programs(2) - 1)
    def _():
        for hh in range(width // HEAD_DIM):
            s_ref[0, hh] = st_scr[hh].T


def hgrn_prompt(proj, lb, ng, b, L, n_heads, col_blocks, rows_per_step=512, chunk=128, leaf=HG_CHUNK,
                heads_per_step=4):
    rows = min(rows_per_step, L)
    hps = heads_per_step
    assert L % rows == 0 and rows % chunk == 0 and n_heads % hps == 0 and all(cb % hps == 0 for cb in col_blocks)
    nc = L // rows
    width = hps * HEAD_DIM

    def in_spec(cb):
        return pl.BlockSpec((rows, width), lambda bi, h, c: (bi * nc + c, cb // hps + h))

    return pl.pallas_call(
        functools.partial(_hgrn_prompt_kernel, chunk=chunk, leaf=leaf),
        out_shape=(jax.ShapeDtypeStruct((b * L, n_heads * HEAD_DIM), BF16),
                   jax.ShapeDtypeStruct((b, n_heads, HEAD_DIM, HEAD_DIM), F32)),
        grid=(b, n_heads // hps, nc),
        in_specs=[in_spec(cb) for cb in col_blocks] + [
            pl.BlockSpec((1, width), lambda bi, h, c: (0, h)),
            pl.BlockSpec((1, HEAD_DIM), lambda bi, h, c: (0, 0))],
        out_specs=(pl.BlockSpec((rows, width), lambda bi, h, c: (bi * nc + c, h)),
                   pl.BlockSpec((1, hps, HEAD_DIM, HEAD_DIM), lambda bi, h, c: (bi, h, 0, 0))),
        scratch_shapes=[pltpu.VMEM((hps, HEAD_DIM, HEAD_DIM), F32)],
        compiler_params=_params("parallel", "parallel", "arbitrary"),
        name="hgrn_prompt",
    )(proj, proj, proj, proj, lb, ng)


def _alibi_slopes(n_pat, n_slots):
    k = jnp.arange(1, n_pat * n_slots + 1, dtype=F32)
    return (2.0 ** (-ALIBI_MAX * k / (n_pat * n_slots))).reshape(n_pat, n_slots)


def _strided_rows(ref, start, size, stride):
    if stride == 1:
        return ref[pl.ds(start, size), :]
    return ref[pl.ds(start, size, stride=stride), :]


def _band_softmax(q, kk, vv, bias, valid, scale):
    s = _dot_nt(q.astype(BF16), kk.astype(BF16)) * scale - bias
    s = jnp.where(valid, s, NEG_INF)
    m = jnp.max(s, axis=-1, keepdims=True)
    p = jnp.exp(s - m)
    l = jnp.sum(p, axis=-1, keepdims=True)
    o = jnp.dot(p.astype(BF16), vv.astype(BF16), preferred_element_type=F32) / l
    return o, m + jnp.log(l)


def _dil_attn_fused_kernel(*refs, patterns):
    n = len(patterns)
    out_ref = refs[6 * n]
    scr = refs[6 * n + 1:]
    for p, (dil, band, units) in enumerate(patterns):
        _dil_attn_kernel(*refs[6 * p:6 * p + 6], scr[2 * p], scr[2 * p + 1], dil=dil, band=band, units=units)
    ls = [scr[2 * p + 1][...] for p in range(n)]
    m = functools.reduce(jnp.maximum, ls)
    ws = [jnp.exp(l - m) for l in ls]
    num = functools.reduce(lambda a, c: a + c, [w * scr[2 * p][...] for p, w in enumerate(ws)])
    den = functools.reduce(lambda a, c: a + c, ws)
    out_ref[...] = (num / den).astype(out_ref.dtype)


def dilated_attention_prompt_fused(proj, slopes, b, L, n_slots, qcol, kcol, vcol, patterns):
    spans = [win for win, dil in patterns]
    rows = max(spans)
    assert L % rows == 0 and all(rows % s == 0 for s in spans)
    nblk = L // rows
    in_specs, args, static = [], [], []
    for p, (win, dil) in enumerate(patterns):
        band = win // dil
        span, units = band * dil, rows // (band * dil)
        static.append((dil, band, units))
        off = p * n_slots
        cur = lambda c0: pl.BlockSpec((rows, HEAD_DIM), lambda bi, h, i, c0=c0: (bi * nblk + i, c0 + h))
        prev = lambda c0, span=span, units=units: pl.BlockSpec(
            (span, HEAD_DIM), lambda bi, h, i: (bi * nblk * units + jnp.maximum(i * units - 1, 0), c0 + h))
        in_specs += [cur(qcol + off), prev(kcol + off), cur(kcol + off), prev(vcol + off), cur(vcol + off),
                     pl.BlockSpec((1, 1, 2 * band), lambda bi, h, i: (h, 0, 0))]
        args += [proj] * 5 + [jnp.broadcast_to(slopes[p].reshape(n_slots, 1, 1), (n_slots, 1, 2 * band)).astype(F32)]
    return pl.pallas_call(
        functools.partial(_dil_attn_fused_kernel, patterns=tuple(static)),
        out_shape=jax.ShapeDtypeStruct((b * L, n_slots * HEAD_DIM), BF16),
        grid=(b, n_slots, nblk),
        in_specs=in_specs,
        out_specs=pl.BlockSpec((rows, HEAD_DIM), lambda bi, h, i: (bi * nblk + i, h)),
        scratch_shapes=[pltpu.VMEM((rows, HEAD_DIM), F32)] * (2 * len(patterns)),
        compiler_params=_params("parallel", "parallel", "arbitrary"),
        name="dil_attn_prompt",
    )(*args)


def _dil_attn_kernel(q_ref, kp_ref, kc_ref, vp_ref, vc_ref, slope_ref, o_ref, lse_ref, *, dil, band, units):
    first = pl.program_id(2) == 0
    span = band * dil
    scale = HEAD_DIM ** -0.5
    qi = lax.broadcasted_iota(jnp.int32, (band, 2 * band), 0)
    kj = lax.broadcasted_iota(jnp.int32, (band, 2 * band), 1)
    off = qi + band - kj
    in_band = (off >= 0) & (off <= band)
    bias = slope_ref[0] * float(dil) * off.astype(F32)
    for u in range(units):
        for r in range(dil):
            base = u * span + r
            q = _strided_rows(q_ref, base, band, dil)
            if u == 0:
                kp, vp = _strided_rows(kp_ref, r, band, dil), _strided_rows(vp_ref, r, band, dil)
                valid = in_band & (jnp.logical_not(first) | (kj >= band))
            else:
                kp, vp = _strided_rows(kc_ref, base - span, band, dil), _strided_rows(vc_ref, base - span, band, dil)
                valid = in_band
            kk = jnp.concatenate([kp, _strided_rows(kc_ref, base, band, dil)], axis=0)
            vv = jnp.concatenate([vp, _strided_rows(vc_ref, base, band, dil)], axis=0)
            o, lse = _band_softmax(q, kk, vv, bias, valid, scale)
            lse_b = jnp.broadcast_to(lse, (band, HEAD_DIM))
            if dil == 1:
                o_ref[pl.ds(base, band), :] = o
                lse_ref[pl.ds(base, band), :] = lse_b
            else:
                o_ref[pl.ds(base, band, stride=dil), :] = o
                lse_ref[pl.ds(base, band, stride=dil), :] = lse_b


def dilated_attention_prompt(proj, slopes_p, b, L, n_slots, qcol, kcol, vcol, win, dil, units):
    band = win // dil
    span = band * dil
    rows = span * units
    assert L % rows == 0
    nblk = L // rows
    cur = lambda c0: pl.BlockSpec((rows, HEAD_DIM), lambda bi, h, i: (bi * nblk + i, c0 + h))
    prev = lambda c0: pl.BlockSpec(
        (span, HEAD_DIM), lambda bi, h, i: (bi * nblk * units + jnp.maximum(i * units - 1, 0), c0 + h))
    out = pl.BlockSpec((rows, HEAD_DIM), lambda bi, h, i: (bi * nblk + i, h))
    shape = jax.ShapeDtypeStruct((b * L, n_slots * HEAD_DIM), F32)
    slope_rep = jnp.broadcast_to(slopes_p.reshape(n_slots, 1, 1), (n_slots, 1, 2 * band)).astype(F32)
    return pl.pallas_call(
        functools.partial(_dil_attn_kernel, dil=dil, band=band, units=units),
        out_shape=(shape, shape),
        grid=(b, n_slots, nblk),
        in_specs=[cur(qcol), prev(kcol), cur(kcol), prev(vcol), cur(vcol),
                  pl.BlockSpec((1, 1, 2 * band), lambda bi, h, i: (h, 0, 0))],
        out_specs=(out, out),
        compiler_params=_params("parallel", "parallel", "arbitrary"),
        name=f"dil_attn_prompt_d{dil}",
    )(proj, proj, proj, proj, proj, slope_rep)


def _merge_kernel(*refs):
    n = (len(refs) - 1) // 2
    o_refs, l_refs, out_ref = refs[:n], refs[n:2 * n], refs[-1]
    ls = [r[...] for r in l_refs]
    m = functools.reduce(jnp.maximum, ls)
    ws = [jnp.exp(l - m) for l in ls]
    num = functools.reduce(lambda a, c: a + c, [w * r[...] for w, r in zip(ws, o_refs)])
    den = functools.reduce(lambda a, c: a + c, ws)
    out_ref[...] = (num / den).astype(out_ref.dtype)


def merge_patterns(outs, lses, tm=512):
    T, W = outs[0].shape
    tm = _row_tile(T, tm)
    spec = pl.BlockSpec((tm, W), lambda i: (i, 0))
    return pl.pallas_call(
        _merge_kernel,
        out_shape=jax.ShapeDtypeStruct((T, W), BF16),
        grid=(T // tm,),
        in_specs=[spec] * (2 * len(outs)),
        out_specs=spec,
        compiler_params=_params("parallel"),
        name="merge_patterns",
    )(*outs, *lses)


def _mem_attn_kernel(q_ref, k_ref, v_ref, o_ref):
    scale = q_ref.shape[-1] ** -0.5
    s = _dot_nt(q_ref[...].astype(BF16), k_ref[...].astype(BF16)) * scale
    m = jnp.max(s, axis=-1, keepdims=True)
    p = jnp.exp(s - m)
    l = jnp.sum(p, axis=-1, keepdims=True)
    o = jnp.dot(p.astype(BF16), v_ref[...].astype(BF16), preferred_element_type=F32) / l
    o_ref[...] = o.astype(o_ref.dtype)


def mem_attention_prompt(q, mk, mv, b, L, M, n_heads, tq=1024):
    D = mk.shape[1]
    hd = D // n_heads
    tq = _row_tile(L, tq)
    nq = L // tq
    return pl.pallas_call(
        _mem_attn_kernel,
        out_shape=jax.ShapeDtypeStruct((b * L, D), BF16),
        grid=(b, n_heads, nq),
        in_specs=[pl.BlockSpec((tq, hd), lambda bi, h, i: (bi * nq + i, h)),
                  pl.BlockSpec((M, hd), lambda bi, h, i: (bi, h)),
                  pl.BlockSpec((M, hd), lambda bi, h, i: (bi, h))],
        out_specs=pl.BlockSpec((tq, hd), lambda bi, h, i: (bi * nq + i, h)),
        compiler_params=_params("parallel", "parallel", "arbitrary"),
        name="mem_attn_prompt",
    )(q, mk, mv)


def _dot_nt_x3(a, b):
    a_hi, b_hi = a.astype(BF16), b.astype(BF16)
    a_lo, b_lo = (a - a_hi.astype(F32)).astype(BF16), (b - b_hi.astype(F32)).astype(BF16)
    return _dot_nt(a_hi, b_hi) + _dot_nt(a_hi, b_lo) + _dot_nt(a_lo, b_hi)


def _topk_rows(x, k, payload=None):
    n = x.shape[0]
    iota = lax.broadcasted_iota(jnp.int32, x.shape, 0).astype(F32)
    vals, outs = [], []
    for _ in range(k):
        m = jnp.max(x, axis=0, keepdims=True)
        pos = jnp.min(jnp.where(x == m, iota, float(n)), axis=0, keepdims=True)
        sel = iota == pos
        vals.append(m)
        outs.append(pos if payload is None else jnp.max(jnp.where(sel, payload, -1.0), axis=0, keepdims=True))
        x = jnp.where(sel, -jnp.inf, x)
    return jnp.concatenate(vals, axis=0), jnp.concatenate(outs, axis=0)


def _candidates(v1, i1, v2, i2, kk):
    vals, idxs, a = [], [], 0
    while a < kk and kk // (a + 1) >= 2:
        nb = min(kk, -(-(kk // (a + 1)) // 8) * 8)
        vals.append(v1[a:a + 1, :] + v2[:nb, :])
        idxs.append(i1[a:a + 1, :] * float(N_KEYS) + i2[:nb, :])
        a += 1
    if a < kk:
        vals.append(v1[a:, :] + v2[0:1, :])
        idxs.append(i1[a:, :] * float(N_KEYS) + i2[0:1, :])
    return jnp.concatenate(vals, axis=0), jnp.concatenate(idxs, axis=0)


_STAGE_PITCH = N_KEYS + 8


def _router_kernel(qp_ref, sk_ref, g_ref, ii_scr, ij_scr, w_scr, iit_scr, ijt_scr, wt_scr, stage_scr):
    tb = qp_ref.shape[0]
    kk = PK_TOPK
    for h in range(PK_HEADS):
        tops = []
        for c in range(2):
            col = (2 * h + c) * HEAD_DIM
            sc = _dot_nt_x3(sk_ref[2 * h + c], qp_ref[:, col:col + HEAD_DIM])
            tops.append(_topk_rows(sc, kk))
        (v1, i1), (v2, i2) = tops
        cand, cidx = _candidates(v1, i1, v2, i2, kk)
        best, eidx = _topk_rows(cand, kk, payload=cidx)
        e = jnp.exp(best - best[0:1, :])
        gate = e / jnp.sum(e, axis=0, keepdims=True)
        ei = jnp.floor(eidx * (1.0 / N_KEYS))
        ii_scr[h * kk:(h + 1) * kk, :] = ei
        ij_scr[h * kk:(h + 1) * kk, :] = eidx - ei * float(N_KEYS)
        w_scr[h * kk:(h + 1) * kk, :] = gate
    iit_scr[...] = ii_scr[...].T
    ijt_scr[...] = ij_scr[...].T
    wt_scr[...] = w_scr[...].T
    sub = lax.broadcasted_iota(jnp.int32, (N_KEYS, PK_HEADS * kk), 0).astype(F32)

    def per_token(t, carry):
        a = jnp.where(sub == iit_scr[pl.ds(t, 1), :], wt_scr[pl.ds(t, 1), :], 0.0).astype(BF16)
        bsel = jnp.where(sub == ijt_scr[pl.ds(t, 1), :], 1.0, 0.0).astype(BF16)
        row0 = pl.multiple_of(t * _STAGE_PITCH, 8)
        stage_scr[pl.ds(row0, N_KEYS), :] = _dot_nt(a, bsel)
        return carry

    lax.fori_loop(0, tb, per_token, 0, unroll=32)
    for i in range(N_KEYS):
        g_ref[:, i * N_KEYS:(i + 1) * N_KEYS] = stage_scr[pl.ds(i, tb, stride=_STAGE_PITCH), :].astype(g_ref.dtype)


def peer_router(qp, subkeys, tb=128):
    T = qp.shape[0]
    assert T % tb == 0 and tb == N_KEYS
    sk = subkeys.reshape(PK_HEADS * 2, N_KEYS, HEAD_DIM)
    slots = PK_HEADS * PK_TOPK
    return pl.pallas_call(
        _router_kernel,
        out_shape=jax.ShapeDtypeStruct((T, N_KEYS * N_KEYS), BF16),
        grid=(T // tb,),
        in_specs=[pl.BlockSpec((tb, qp.shape[1]), lambda i: (i, 0)),
                  pl.BlockSpec(sk.shape, lambda i: (0, 0, 0))],
        out_specs=pl.BlockSpec((tb, N_KEYS * N_KEYS), lambda i: (i, 0)),
        scratch_shapes=[pltpu.VMEM((slots, tb), F32)] * 3 + [pltpu.VMEM((tb, slots), F32)] * 3
        + [pltpu.VMEM((tb * _STAGE_PITCH, N_KEYS), F32)],
        compiler_params=_params("parallel"),
        name="peer_router",
    )(qp, sk)


def _gelu(x):
    return 0.5 * x * (1.0 + lax.erf(x * (2.0 ** -0.5)))


def _cast_kernel(x_ref, o_ref):
    o_ref[...] = x_ref[...].astype(o_ref.dtype)


def cast(x, dtype, tm=512):
    R, C = x.shape
    tm = _row_tile(R, tm)
    spec = pl.BlockSpec((tm, C), lambda i: (i, 0))
    return pl.pallas_call(
        _cast_kernel, out_shape=jax.ShapeDtypeStruct((R, C), dtype), grid=(R // tm,), in_specs=[spec],
        out_specs=spec, compiler_params=_params("parallel"), name="cast",
    )(x)


def _peer_kernel(n_ref, g_ref, u_ref, v_ref, o_ref, hid_a, hid_b):
    j = pl.program_id(1)

    @pl.when(j == 0)
    def _():
        hid_b[...] = jnp.zeros_like(hid_b)
        o_ref[...] = jnp.zeros_like(o_ref)

    def step(prev_scr, next_scr):
        part = jnp.dot(prev_scr[...], v_ref[...].astype(BF16), preferred_element_type=F32)
        xu = _dot_nt(n_ref[...], u_ref[...].astype(BF16))
        next_scr[...] = (g_ref[...].astype(F32) * _gelu(xu)).astype(BF16)
        o_ref[...] += part

    @pl.when(j % 2 == 0)
    def _():
        step(hid_b, hid_a)

    @pl.when(j % 2 == 1)
    def _():
        step(hid_a, hid_b)


def peer_experts(n, gates, u, v, tm=1088, te=256):
    T, D = n.shape
    E = u.shape[0]
    tm = _row_tile(T, tm)
    assert E % te == 0
    ne = E // te
    resident = pl.Buffered(1)
    return pl.pallas_call(
        _peer_kernel,
        out_shape=jax.ShapeDtypeStruct((T, D), F32),
        grid=(T // tm, ne + 1),
        in_specs=[pl.BlockSpec((tm, D), lambda i, j: (i, 0), pipeline_mode=resident),
                  pl.BlockSpec((tm, te), lambda i, j: (i, jnp.minimum(j, ne - 1))),
                  pl.BlockSpec((te, D), lambda i, j: (jnp.minimum(j, ne - 1), 0)),
                  pl.BlockSpec((te, D), lambda i, j: (jnp.maximum(j - 1, 0), 0))],
        out_specs=pl.BlockSpec((tm, D), lambda i, j: (i, 0), pipeline_mode=resident),
        scratch_shapes=[pltpu.VMEM((tm, te), BF16)] * 2,
        compiler_params=_params("parallel", "arbitrary"),
        name="peer_experts",
    )(n, gates, u, v)


def _hgrn_sample_kernel(hq_ref, hf_ref, hi_ref, hg_ref, lb_ref, ng_ref, s_ref, o_ref, so_ref, *, n_tok):
    rows, width = hq_ref.shape
    n_seq, n_head = rows // n_tok, width // HEAD_DIM
    q, k, g = _hgrn_gates(hq_ref[...], hf_ref[...], lb_ref[...])
    t = lax.broadcasted_iota(jnp.int32, (rows, width), 0) % n_tok
    cum, sh = g, 1
    while sh < n_tok:
        cum = cum + jnp.where(t >= sh, pltpu.roll(cum, sh, axis=0), 0.0)
        sh *= 2
    last = _rows_bcast(cum, [s * n_tok + n_tok - 1 for s in range(n_seq)], n_tok)
    q_in = q * jnp.exp(cum)
    k_in = k * jnp.exp(-cum)
    k_out = k * jnp.exp(last - cum)
    dec_t = jnp.exp(last).T
    v = hi_ref[...]
    it = lax.broadcasted_iota(jnp.int32, (n_tok, n_tok), 0)
    js = lax.broadcasted_iota(jnp.int32, (n_tok, n_tok), 1)
    ng = ng_ref[...]
    for s in range(n_seq):
        r = slice(s * n_tok, (s + 1) * n_tok)
        for h in range(n_head):
            c = slice(h * HEAD_DIM, (h + 1) * HEAD_DIM)
            st = s_ref[0, s, h]
            qh, vh = q_in[r, c].astype(BF16), v[r, c].astype(BF16)
            a = jnp.where(it >= js, _dot_nt(qh, k_in[r, c].astype(BF16)), 0.0)
            o = jnp.dot(a.astype(BF16), vh, preferred_element_type=F32)
            o = o + jnp.dot(qh, st.astype(BF16), preferred_element_type=F32)
            dec = dec_t[c, s * n_tok:s * n_tok + 1]
            so_ref[0, s, h] = st * dec + _dot_tn(k_out[r, c].astype(BF16), vh)
            o_ref[r, c] = _hgrn_out(o, hg_ref[r, c], ng).astype(o_ref.dtype)


def hgrn_sample(proj, row0, state, lb, ng, n_tok, col_blocks, seqs_per_step=4, heads_per_step=4):
    _, n_seq, n_heads, _, _ = state.shape
    rows, width = seqs_per_step * n_tok, heads_per_step * HEAD_DIM
    assert n_seq % seqs_per_step == 0 and n_heads % heads_per_step == 0 and row0 % rows == 0
    r0, nhb = row0 // rows, n_heads // heads_per_step

    def in_spec(cb):
        return pl.BlockSpec((rows, width), lambda i, j: (r0 + i, cb // heads_per_step + j))

    st_spec = pl.BlockSpec((1, seqs_per_step, heads_per_step, HEAD_DIM, HEAD_DIM), lambda i, j: (0, i, j, 0, 0))
    return pl.pallas_call(
        functools.partial(_hgrn_sample_kernel, n_tok=n_tok),
        out_shape=(jax.ShapeDtypeStruct((n_seq * n_tok, n_heads * HEAD_DIM), BF16),
                   jax.ShapeDtypeStruct(state.shape, F32)),
        grid=(n_seq // seqs_per_step, nhb),
        in_specs=[in_spec(cb) for cb in col_blocks] + [
            pl.BlockSpec((1, width), lambda i, j: (0, j)),
            pl.BlockSpec((1, HEAD_DIM), lambda i, j: (0, 0)),
            st_spec],
        out_specs=(pl.BlockSpec((rows, width), lambda i, j: (i, j)), st_spec),
        compiler_params=_params("parallel", "parallel"),
        name="hgrn_sample",
    )(proj, proj, proj, proj, lb, ng, state)


def _lane_sum_rep(x):
    ones = jnp.ones((x.shape[1], HEAD_DIM), BF16)
    hi = x.astype(BF16)
    lo = (x - hi.astype(F32)).astype(BF16)
    return jnp.dot(hi, ones, preferred_element_type=F32) + jnp.dot(lo, ones, preferred_element_type=F32)


def _dil_attn_sample_kernel(q_ref, kn_ref, vn_ref, ck_ref, cv_ref, ckx_ref, cvx_ref, slope_ref,
                            o_ref, lse_ref, ok_ref, ov_ref, m_scr, l_scr, acc_scr, *, dil, band):
    n, H, D = q_ref.shape[1:]
    rc = ck_ref.shape[2]
    c, nc = pl.program_id(1), pl.num_programs(1)
    scale = D ** -0.5
    slope = slope_ref[...] * float(dil)

    last = c == nc - 1
    for src, nxt, new, dst in ((ck_ref, ckx_ref, kn_ref, ok_ref), (cv_ref, cvx_ref, vn_ref, ov_ref)):
        dst[0, 0, 0:rc - n] = src[0, 0, n:rc]
        dst[0, 0, rc - n:rc] = jnp.where(last, new[0], nxt[0, 0])

    @pl.when(c == 0)
    def _():
        for i in range(n):
            q = q_ref[0, i]
            news = [ip for ip in range(i + 1) if (i - ip) % dil == 0]
            s_n = [_lane_sum_rep(kn_ref[0, ip] * q) * scale - slope * float((i - ip) // dil) for ip in news]
            m = functools.reduce(jnp.maximum, s_n)
            p_n = [jnp.exp(s - m) for s in s_n]
            m_scr[i] = m
            l_scr[i] = functools.reduce(lambda a, b: a + b, p_n)
            acc_scr[i] = functools.reduce(lambda a, b: a + b, [p * vn_ref[0, ip] for p, ip in zip(p_n, news)])

    rows = rc // dil if dil >= n else band
    mm = lax.broadcasted_iota(jnp.int32, (rows, H, D), 0)
    for i in range(n):
        q = q_ref[0, i]
        if dil >= n:
            kc = ck_ref[0, 0, pl.ds(i, rows, stride=dil)]
            vc = cv_ref[0, 0, pl.ds(i, rows, stride=dil)]
            jc = (nc - c) * rows - mm
            ok_c = None
        else:
            kc = ck_ref[0, 0, pl.ds(rc - band, band)]
            vc = cv_ref[0, 0, pl.ds(rc - band, band)]
            jc = band + i - mm
            ok_c = jc <= band
        s_c = _lane_sum_rep((kc * q[None]).reshape(rows * H, D)).reshape(rows, H, D) * scale
        s_c = s_c - slope[None] * jc.astype(F32)
        if ok_c is not None:
            s_c = jnp.where(ok_c, s_c, NEG_INF)
        m_old = m_scr[i]
        m_new = jnp.maximum(m_old, jnp.max(s_c, axis=0))
        alpha = jnp.exp(m_old - m_new)
        p_c = jnp.exp(s_c - m_new[None])
        l_scr[i] = l_scr[i] * alpha + jnp.sum(p_c, axis=0)
        acc_scr[i] = acc_scr[i] * alpha + jnp.sum(p_c * vc, axis=0)
        m_scr[i] = m_new

    @pl.when(last)
    def _():
        for i in range(n):
            o_ref[0, i] = acc_scr[i] / l_scr[i]
            lse_ref[0, i] = m_scr[i] + jnp.log(l_scr[i])


def dilated_attention_sample(q, k_new, v_new, cache_k, cache_v, slopes_p, win, dil, chunk_rows=512):
    B, n, H, D = q.shape
    keep = cache_k.shape[2]
    band = win // dil
    assert keep == band * dil and keep > n and keep % n == 0, "window cache must hold exactly one window"
    rc = keep if dil < n else min(keep, chunk_rows)
    assert keep % rc == 0 and rc % dil == 0 and rc % n == 0 and (dil >= n or dil == 1)
    nc = keep // rc
    new_spec = pl.BlockSpec((1, n, H, D), lambda b, c: (b, 0, 0, 0))
    c_spec = pl.BlockSpec((1, 1, rc, H, D), lambda b, c: (0, b, c, 0, 0))
    x_spec = pl.BlockSpec((1, 1, n, H, D), lambda b, c: (0, b, jnp.minimum((c + 1) * (rc // n), keep // n - 1), 0, 0))
    shape = jax.ShapeDtypeStruct((B, n, H, D), F32)
    slope_rep = jnp.broadcast_to(slopes_p.reshape(H, 1), (H, D)).astype(F32)
    return pl.pallas_call(
        functools.partial(_dil_attn_sample_kernel, dil=dil, band=band),
        out_shape=(shape, shape, jax.ShapeDtypeStruct(cache_k.shape, cache_k.dtype),
                   jax.ShapeDtypeStruct(cache_v.shape, cache_v.dtype)),
        grid=(B, nc),
        in_specs=[new_spec, new_spec, new_spec, c_spec, c_spec, x_spec, x_spec,
                  pl.BlockSpec((H, D), lambda b, c: (0, 0))],
        out_specs=(new_spec, new_spec, c_spec, c_spec),
        scratch_shapes=[pltpu.VMEM((n, H, D), F32)] * 3,
        compiler_params=_params("parallel", "arbitrary"),
        name=f"dil_attn_sample_d{dil}",
    )(q, k_new, v_new, cache_k, cache_v, cache_k, cache_v, slope_rep)


def _mem_attn_sample_kernel(q_ref, k_ref, v_ref, o_ref):
    n, H, D = q_ref.shape[1:]
    M = k_ref.shape[2]
    scale = D ** -0.5
    k2 = k_ref[0, 0].reshape(M * H, D).astype(BF16)
    v2 = v_ref[0, 0].reshape(M * H, D).astype(BF16)
    q2 = q_ref[0].reshape(n * H, D).astype(BF16)
    s = _dot_nt(q2, k2) * scale
    qh = lax.broadcasted_iota(jnp.int32, s.shape, 0) % H
    kh = lax.broadcasted_iota(jnp.int32, s.shape, 1) % H
    s = jnp.where(qh == kh, s, NEG_INF)
    m = jnp.max(s, axis=-1, keepdims=True)
    p = jnp.exp(s - m)
    l = jnp.sum(p, axis=-1, keepdims=True)
    o = jnp.dot(p.astype(BF16), v2, preferred_element_type=F32) / l
    o_ref[0] = o.reshape(n, H, D)


def mem_attention_sample(q, mem_k, mem_v):
    B, n, H, D = q.shape
    M = mem_k.shape[2]
    q_spec = pl.BlockSpec((1, n, H, D), lambda i: (i, 0, 0, 0))
    m_spec = pl.BlockSpec((1, 1, M, H, D), lambda i: (0, i, 0, 0, 0))
    return pl.pallas_call(
        _mem_attn_sample_kernel,
        out_shape=jax.ShapeDtypeStruct((B, n, H, D), F32),
        grid=(B,),
        in_specs=[q_spec, m_spec, m_spec],
        out_specs=q_spec,
        compiler_params=_params("parallel"),
        name="mem_attn_sample",
    )(q, mem_k, mem_v)


def _shift_copies(ck_ref, cv_ref, nk_ref, nv_ref, zero_scr, sems, n_new):
    keep = ck_ref.shape[2]
    moves = [pltpu.make_async_copy(src.at[:, :, pl.ds(n_new, keep - n_new)], dst.at[:, :, pl.ds(0, keep - n_new)],
                                   sems.at[s]) for s, (src, dst) in enumerate(((ck_ref, nk_ref), (cv_ref, nv_ref)))]
    fills = [pltpu.make_async_copy(zero_scr, dst.at[:, :, pl.ds(keep - n_new, n_new)], sems.at[2 + s])
             for s, dst in enumerate((nk_ref, nv_ref))]
    return moves + fills


def _mm_shift_kernel(a_ref, w_ref, ck_ref, cv_ref, o_ref, nk_ref, nv_ref, zero_scr, sems, *, n_new):
    i, j = pl.program_id(0), pl.program_id(1)

    @pl.when((i == 0) & (j == 0))
    def _():
        zero_scr[...] = jnp.zeros_like(zero_scr)
        for c in _shift_copies(ck_ref, cv_ref, nk_ref, nv_ref, zero_scr, sems, n_new):
            c.start()

    o_ref[...] = jnp.dot(a_ref[...], w_ref[...].astype(BF16), preferred_element_type=F32)

    @pl.when((i == pl.num_programs(0) - 1) & (j == pl.num_programs(1) - 1))
    def _():
        for c in _shift_copies(ck_ref, cv_ref, nk_ref, nv_ref, zero_scr, sems, n_new):
            c.wait()


def matmul_with_cache_shift(a, w, cache_k, cache_v, n_new, tm=1088, tn=512):
    M, K = a.shape
    N = w.shape[1]
    tm, tn = _row_tile(M, tm), min(tn, N)
    assert N % tn == 0
    any_spec = pl.BlockSpec(memory_space=pl.ANY)
    return pl.pallas_call(
        functools.partial(_mm_shift_kernel, n_new=n_new),
        out_shape=(jax.ShapeDtypeStruct((M, N), F32), jax.ShapeDtypeStruct(cache_k.shape, cache_k.dtype),
                   jax.ShapeDtypeStruct(cache_v.shape, cache_v.dtype)),
        grid=(M // tm, N // tn),
        in_specs=[pl.BlockSpec((tm, K), lambda i, j: (i, 0)), pl.BlockSpec((K, tn), lambda i, j: (0, j)),
                  any_spec, any_spec],
        out_specs=(pl.BlockSpec((tm, tn), lambda i, j: (i, j)), any_spec, any_spec),
        scratch_shapes=[pltpu.VMEM(cache_k.shape[:2] + (n_new,) + cache_k.shape[3:], cache_k.dtype),
                        pltpu.SemaphoreType.DMA((4,))],
        compiler_params=_params("arbitrary", "arbitrary"),
        name="matmul_cache_shift",
    )(a, w, cache_k, cache_v)


def _dil_attn_sample_strided_kernel(q_ref, kn_ref, vn_ref, ck_ref, cv_ref, slope_ref, nk_in, nv_in,
                                    o_ref, lse_ref, ok_ref, ov_ref, *, dil):
    del nk_in, nv_in
    n, H, D = q_ref.shape[1:]
    band = ck_ref.shape[2]
    scale = D ** -0.5
    slope = slope_ref[...] * float(dil)
    jc = (band - lax.broadcasted_iota(jnp.int32, (band, H, D), 0)).astype(F32)
    ok_ref[0, 0] = kn_ref[0]
    ov_ref[0, 0] = vn_ref[0]
    for i in range(n):
        q = q_ref[0, i]
        kc, vc = ck_ref[0, 0, :, i], cv_ref[0, 0, :, i]
        s_c = _lane_sum_rep((kc * q[None]).reshape(band * H, D)).reshape(band, H, D) * scale - slope[None] * jc
        s_n = _lane_sum_rep(kn_ref[0, i] * q) * scale
        m = jnp.maximum(jnp.max(s_c, axis=0), s_n)
        p_c, p_n = jnp.exp(s_c - m[None]), jnp.exp(s_n - m)
        l = jnp.sum(p_c, axis=0) + p_n
        o_ref[0, i] = (jnp.sum(p_c * vc, axis=0) + p_n * vn_ref[0, i]) / l
        lse_ref[0, i] = m + jnp.log(l)


def dilated_attention_sample_strided(q, k_new, v_new, cache_k, cache_v, shifted_k, shifted_v, slopes_p, win, dil):
    B, n, H, D = q.shape
    keep = cache_k.shape[2]
    band = win // dil
    assert keep == band * dil and dil >= n and keep % n == 0
    new_spec = pl.BlockSpec((1, n, H, D), lambda b: (b, 0, 0, 0))
    res_spec = pl.BlockSpec((1, 1, band, n, H, D), lambda b: (0, b, 0, 0, 0, 0))
    tail_spec = pl.BlockSpec((1, 1, n, H, D), lambda b: (0, b, keep // n - 1, 0, 0))
    any_spec = pl.BlockSpec(memory_space=pl.ANY)
    shape = jax.ShapeDtypeStruct((B, n, H, D), F32)
    slope_rep = jnp.broadcast_to(slopes_p.reshape(H, 1), (H, D)).astype(F32)
    view = lambda c: c.reshape(1, B, band, dil, H, D)
    return pl.pallas_call(
        functools.partial(_dil_attn_sample_strided_kernel, dil=dil),
        out_shape=(shape, shape, jax.ShapeDtypeStruct(shifted_k.shape, shifted_k.dtype),
                   jax.ShapeDtypeStruct(shifted_v.shape, shifted_v.dtype)),
        grid=(B,),
        in_specs=[new_spec, new_spec, new_spec, res_spec, res_spec, pl.BlockSpec((H, D), lambda b: (0, 0)),
                  any_spec, any_spec],
        out_specs=(new_spec, new_spec, tail_spec, tail_spec),
        input_output_aliases={6: 2, 7: 3},
        compiler_params=_params("arbitrary"),
        name=f"dil_attn_sample_strided_d{dil}",
    )(q, k_new, v_new, view(cache_k), view(cache_v), slope_rep, shifted_k, shifted_v)


def kernel(x_prompt, x_sample, mem_prompt, state_hgrn, cache_w1_k, cache_w1_v, cache_w2_k, cache_w2_v, cache_w3_k, cache_w3_v, cache_mem_k, cache_mem_v, norm_mix_g, w_in, hg_lower_bound, hg_norm_g, w_out, norm_x_g, norm_mem_g, wq_x, wk_x, wv_x, wo_x, norm_ffn_g, peer_wq, peer_subkeys, peer_u, peer_v, norm_final_g):
    b, L, D = x_prompt.shape
    db, dn, _ = x_sample.shape
    depth = w_in.shape[0]
    n_p, n_s = b * L, db * dn
    hg_heads = state_hgrn.shape[2]
    n_slots = cache_w1_k.shape[3]
    n_pat = len(DIL_PATTERNS)
    M = mem_prompt.shape[1]
    hg_cols = (0, hg_heads, 2 * hg_heads, 3 * hg_heads)
    qcol, kcol, vcol = (4 * hg_heads + s * n_pat * n_slots for s in range(3))
    lbs = jnp.cumsum(jax.nn.softmax(hg_lower_bound.astype(F32), axis=0), axis=0)
    slopes = _alibi_slopes(n_pat, n_slots)
    cache_k = (cache_w1_k, cache_w2_k, cache_w3_k)
    cache_v = (cache_w1_v, cache_w2_v, cache_w3_v)

    n1, h = rmsnorm_cat(x_prompt.reshape(n_p, D), x_sample.reshape(n_s, D), norm_mix_g[0])
    hgp, hgs, mkp, mvp = [], [], [], []
    wkp, wvp, wks, wvs = ([[] for _ in DIL_PATTERNS] for _ in range(4))
    for l in range(depth):
        lb, ng = lbs[l].reshape(1, -1), hg_norm_g[l].reshape(1, -1)
        big = n_pat - 1
        proj, sh_k, sh_v = matmul_with_cache_shift(n1 if l == 0 else rmsnorm(h, norm_mix_g[l]), w_in[l],
                                                   cache_k[big][l:l + 1], cache_v[big][l:l + 1], dn)

        def head_cols(col0, p, rows):
            c0 = (col0 + p * n_slots) * HEAD_DIM
            return rows[:, c0:c0 + n_slots * HEAD_DIM]

        o_hg_p, s_p = hgrn_prompt(proj, lb, ng, b, L, hg_heads, hg_cols)
        o_at_p = dilated_attention_prompt_fused(proj, slopes, b, L, n_slots, qcol, kcol, vcol, DIL_PATTERNS)
        for p, (win, dil) in enumerate(DIL_PATTERNS):
            keep = min(win, L)
            for col0, dst in ((kcol, wkp), (vcol, wvp)):
                tails = [head_cols(col0, p, proj[(bi + 1) * L - keep:(bi + 1) * L]) for bi in range(b)]
                dst[p].append(jnp.stack(tails).reshape(b, keep, n_slots, HEAD_DIM))
        hgp.append(s_p)

        o_hg_s, s_s = hgrn_sample(proj, n_p, state_hgrn[l:l + 1], lb, ng, dn, hg_cols)
        hgs.append(s_s[0])
        proj_s = proj[n_p:]
        outs, lses = [], []
        for p, (win, dil) in enumerate(DIL_PATTERNS):
            q_s, k_s, v_s = (head_cols(c, p, proj_s).reshape(db, dn, n_slots, HEAD_DIM) for c in (qcol, kcol, vcol))
            ck, cv = cache_k[p][l:l + 1], cache_v[p][l:l + 1]
            if p == big:
                o, lse, nk, nv = dilated_attention_sample_strided(q_s, k_s, v_s, ck, cv, sh_k, sh_v, slopes[p], win, dil)
            else:
                o, lse, nk, nv = dilated_attention_sample(q_s, k_s, v_s, ck, cv, slopes[p], win, dil)
            outs.append(o.reshape(n_s, n_slots * HEAD_DIM))
            lses.append(lse.reshape(n_s, n_slots * HEAD_DIM))
            wks[p].append(nk[0])
            wvs[p].append(nv[0])
        o_at_s = merge_patterns(outs, lses)

        mix_hg = jnp.concatenate([o_hg_p, o_hg_s], axis=0)
        mix_at = jnp.concatenate([o_at_p, o_at_s], axis=0)
        h = matmul([mix_hg, mix_at], w_out[l], res=h)

        nm = rmsnorm(mem_prompt.reshape(b * M, D), norm_mem_g[l])
        mk, mv = matmul([nm], wk_x[l]), matmul([nm], wv_x[l])
        mkp.append(mk.reshape(b, M, MEM_HEADS, D // MEM_HEADS))
        mvp.append(mv.reshape(b, M, MEM_HEADS, D // MEM_HEADS))
        qx = matmul([rmsnorm(h, norm_x_g[l])], wq_x[l])
        a_p = mem_attention_prompt(qx, mk, mv, b, L, M, MEM_HEADS)
        a_s = mem_attention_sample(qx[n_p:].reshape(db, dn, MEM_HEADS, D // MEM_HEADS),
                                   cache_mem_k[l:l + 1], cache_mem_v[l:l + 1])
        att = jnp.concatenate([a_p, a_s.reshape(n_s, D).astype(BF16)], axis=0)
        h = matmul([att], wo_x[l], res=h)

        n3 = rmsnorm(h, norm_ffn_g[l])
        gates = peer_router(matmul([n3], peer_wq[l]), peer_subkeys[l])
        ffn = peer_experts(n3, gates, peer_u[l], peer_v[l])
        if l + 1 < depth:
            h = add_rmsnorm(h, ffn, None)
    y_p, y_s = add_rmsnorm_split(h, ffn, norm_final_g, n_p)
    y_prompt, y_sample = y_p.reshape(b, L, D), y_s.reshape(db, dn, D)
    st = lambda xs: jnp.stack(xs)
    return (y_prompt, y_sample, st(hgp), st(wkp[0]), st(wvp[0]), st(wkp[1]), st(wvp[1]), st(wkp[2]), st(wvp[2]),
            st(mkp), st(mvp), st(hgs), st(wks[0]), st(wvs[0]), st(wks[1]), st(wvs[1]), st(wks[2]), st(wvs[2]))
```
